```python
import jax, jax.numpy as jnp
from jax import lax
import numpy as np

D_MODEL = 4096
BATCH = 8
SEQ = 2048
DEPTH = 2

N_EVEN = (DEPTH + 1) // 2
N_ODD = DEPTH // 2

MIX_WIDTH = D_MODEL
RWKV_WIDTH = MIX_WIDTH // 2
RWKV_HEAD = 64
RWKV_HEADS = RWKV_WIDTH // RWKV_HEAD
DECAY_LORA = 128
ICLR_LORA = 128
SB_WIDTH = MIX_WIDTH - RWKV_WIDTH
SB_HEAD = 128
SB_HEADS = SB_WIDTH // SB_HEAD
SB_BLOCK = 128
SGU_WIDTH = MIX_WIDTH
SGU_CHUNK = 128
SGU_GROUPS = 16
SGU_GROUP_DIM = SGU_WIDTH // SGU_GROUPS

RMS_EPS = 1e-6
GN_EPS = 64e-5
LN_EPS = 1e-5
L2_EPS = 1e-12

RWKV_SHIFT_COLS = 3 * RWKV_WIDTH + DECAY_LORA + ICLR_LORA
EVEN_IN_COLS = RWKV_SHIFT_COLS + RWKV_WIDTH + 4 * SB_WIDTH
ODD_IN_COLS = 3 * SGU_WIDTH

kernel_name = 'hybrid_rwkv7_stickbreak_chunksgu'


def rmsnorm(x, g):
    xf = x.astype(jnp.float32)
    y = xf * lax.rsqrt(jnp.mean(xf * xf, axis=-1, keepdims=True) + RMS_EPS)
    return (y * g.astype(jnp.float32)).astype(x.dtype)


def token_shift(p):
    return jnp.pad(p[:, :-1], ((0, 0), (1, 0), (0, 0)))


def rwkv7_step(state, inp):
    r_t, w_t, k_t, v_t, kk_t, a_t = inp
    sa = jnp.einsum('bhvk,bhk->bhv', state, -kk_t)
    state = (state * w_t[:, :, None, :]
             + sa[..., None] * (kk_t * a_t)[:, :, None, :]
             + v_t[..., None] * k_t[:, :, None, :])
    y = jnp.einsum('bhvk,bhk->bhv', state, r_t)
    return state, y


def rwkv7_mix(p, w_dec_up, w0, a_up, a0, k_k, k_a, r_k, gn_g, gn_b):
    f32 = jnp.float32
    B, S, _ = p.shape
    r, k, v, w_lo, a_lo = jnp.split(
        p.astype(f32),
        [RWKV_WIDTH, 2 * RWKV_WIDTH, 3 * RWKV_WIDTH, 3 * RWKV_WIDTH + DECAY_LORA], axis=-1)
    w_log = -jax.nn.softplus(-(w0.astype(f32) + jnp.tanh(w_lo) @ w_dec_up.astype(f32))) - 0.5
    decay = jnp.exp(-jnp.exp(w_log))
    a = jax.nn.sigmoid(a0.astype(f32) + a_lo @ a_up.astype(f32))

    def heads(t):
        return t.reshape(B, S, RWKV_HEADS, RWKV_HEAD)

    kk = heads(k * k_k.astype(f32))
    kk = kk / jnp.maximum(jnp.sqrt(jnp.sum(kk * kk, axis=-1, keepdims=True)), L2_EPS)
    k = k * (1.0 + (a - 1.0) * k_a.astype(f32))
    r_h, k_h, v_h, w_h, a_h = heads(r), heads(k), heads(v), heads(decay), heads(a)

    xs = tuple(jnp.moveaxis(t, 1, 0) for t in (r_h, w_h, k_h, v_h, kk, a_h))
    state0 = jnp.zeros((B, RWKV_HEADS, RWKV_HEAD, RWKV_HEAD), f32)
    _, ys = lax.scan(rwkv7_step, state0, xs)
    y = jnp.moveaxis(ys, 0, 1)

    mu = jnp.mean(y, axis=-1, keepdims=True)
    var = jnp.mean(jnp.square(y - mu), axis=-1, keepdims=True)
    y = ((y - mu) * lax.rsqrt(var + GN_EPS)).reshape(B, S, RWKV_WIDTH)
    y = y * gn_g.astype(f32) + gn_b.astype(f32)
    bonus = jnp.sum(r_h * k_h * r_k.astype(f32).reshape(RWKV_HEADS, RWKV_HEAD),
                    axis=-1, keepdims=True) * v_h
    return y + bonus.reshape(B, S, RWKV_WIDTH)


def stick_breaking_attention(q, k, v):
    B, S, _ = q.shape

    def heads(t):
        return jnp.transpose(t.reshape(B, S, SB_HEADS, SB_HEAD), (0, 2, 1, 3))

    qh, kh, vh = heads(q), heads(k), heads(v)
    scale = 1.0 / np.sqrt(SB_HEAD)
    outs = []
    for blk in range(S // SB_BLOCK):
        q0 = blk * SB_BLOCK
        k_end = q0 + SB_BLOCK
        z = jnp.einsum('bhtd,bhsd->bhts', qh[:, :, q0:k_end], kh[:, :, :k_end]).astype(jnp.float32) * scale
        t_pos = q0 + jnp.arange(SB_BLOCK)[:, None]
        s_pos = jnp.arange(k_end)[None, :]
        causal = s_pos < t_pos
        log_keep = jnp.where(causal, jax.nn.log_sigmoid(-z), 0.0)
        later = lax.cumsum(log_keep, axis=3, reverse=True) - log_keep
        att = jnp.where(causal, jnp.exp(jax.nn.log_sigmoid(z) + later), 0.0)
        outs.append(jnp.einsum('bhts,bhsd->bhtd', att.astype(vh.dtype), vh[:, :, :k_end]))
    o = jnp.concatenate(outs, axis=2)
    return jnp.transpose(o, (0, 2, 1, 3)).reshape(B, S, SB_WIDTH)


def even_layer(h, w_in, shift_mu, w_dec_up, w0, a_up, a0, k_k, k_a, r_k, gn_g, gn_b, w_out):
    p = h @ w_in
    s1 = RWKV_SHIFT_COLS
    s2 = s1 + RWKV_WIDTH
    s3 = s2 + SB_WIDTH
    s4 = s3 + SB_WIDTH
    s5 = s4 + SB_WIDTH
    p_rwkv, g_rwkv, q, k, v, g_sb = jnp.split(p, [s1, s2, s3, s4, s5], axis=-1)
    p_rwkv = p_rwkv + shift_mu * (token_shift(p_rwkv) - p_rwkv)
    y_a = rwkv7_mix(p_rwkv, w_dec_up, w0, a_up, a0, k_k, k_a, r_k, gn_g, gn_b)
    y_a = y_a * jax.nn.silu(g_rwkv.astype(jnp.float32))
    y_b = stick_breaking_attention(q, k, v).astype(jnp.float32) * jax.nn.silu(g_sb.astype(jnp.float32))
    y = jnp.concatenate([y_a, y_b], axis=-1).astype(h.dtype)
    return (y @ w_out).astype(h.dtype)


def odd_layer(h, w_in, ln_g, ln_b, w_s, b_s, w_out):
    B, S, _ = h.shape
    u, v, g = jnp.split(h @ w_in, 3, axis=-1)
    u = jax.nn.gelu(u.astype(jnp.float32), approximate=False)
    v = jax.nn.gelu(v.astype(jnp.float32), approximate=False)
    mu = jnp.mean(v, axis=-1, keepdims=True)
    var = jnp.mean(jnp.square(v - mu), axis=-1, keepdims=True)
    v = (v - mu) * lax.rsqrt(var + LN_EPS) * ln_g.astype(jnp.float32) + ln_b.astype(jnp.float32)
    vc = v.reshape(B, S // SGU_CHUNK, SGU_CHUNK, SGU_GROUPS, SGU_GROUP_DIM)
    causal = jnp.tril(jnp.ones((SGU_CHUNK, SGU_CHUNK), dtype=bool))
    ws = jnp.where(causal[None], w_s.astype(jnp.float32), 0.0)
    mixed = jnp.einsum('gts,bcsgd->bctgd', ws, vc) + b_s.astype(jnp.float32).T[:, :, None]
    mixed = mixed.reshape(B, S, SGU_WIDTH)
    y = (u * mixed * jax.nn.silu(g.astype(jnp.float32))).astype(h.dtype)
    return (y @ w_out).astype(h.dtype)


def setup_inputs(seed: int = 0) -> dict:
    key = jax.random.key(seed)
    ks = jax.random.split(key, 24)
    f32 = jnp.float32
    W = RWKV_WIDTH

    def nrm(k, shape, s):
        return jax.random.normal(k, shape, f32) * s

    return {
        'x': jax.random.normal(ks[0], (BATCH, SEQ, D_MODEL), f32),
        'norm_g': 1.0 + nrm(ks[1], (DEPTH, D_MODEL), 0.02),
        'final_norm_g': 1.0 + nrm(ks[2], (D_MODEL,), 0.02),
        'e_w_in': nrm(ks[3], (N_EVEN, D_MODEL, EVEN_IN_COLS), D_MODEL ** -0.5),
        'e_shift_mu': jax.random.uniform(ks[4], (N_EVEN, RWKV_SHIFT_COLS), f32),
        'e_w_decay_up': nrm(ks[5], (N_EVEN, DECAY_LORA, W), 0.5 * DECAY_LORA ** -0.5),
        'e_w0': jax.random.uniform(ks[6], (N_EVEN, W), f32, -6.0, -1.0),
        'e_a_up': nrm(ks[7], (N_EVEN, ICLR_LORA, W), 0.5 * ICLR_LORA ** -0.5),
        'e_a0': nrm(ks[8], (N_EVEN, W), 0.1),
        'e_k_k': 0.85 + nrm(ks[9], (N_EVEN, W), 0.05),
        'e_k_a': 1.0 + nrm(ks[10], (N_EVEN, W), 0.05),
        'e_r_k': nrm(ks[11], (N_EVEN, W), 0.1),
        'e_gn_g': 1.0 + nrm(ks[12], (N_EVEN, W), 0.02),
        'e_gn_b': nrm(ks[13], (N_EVEN, W), 0.02),
        'e_w_out': nrm(ks[14], (N_EVEN, MIX_WIDTH, D_MODEL), MIX_WIDTH ** -0.5),
        'o_w_in': nrm(ks[15], (N_ODD, D_MODEL, ODD_IN_COLS), D_MODEL ** -0.5),
        'o_ln_g': 1.0 + nrm(ks[16], (N_ODD, SGU_WIDTH), 0.02),
        'o_ln_b': nrm(ks[17], (N_ODD, SGU_WIDTH), 0.02),
        'o_w_s': nrm(ks[18], (N_ODD, SGU_GROUPS, SGU_CHUNK, SGU_CHUNK), 0.5 * SGU_CHUNK ** -0.5),
        'o_b_s': 1.0 + nrm(ks[19], (N_ODD, SGU_GROUPS, SGU_CHUNK), 0.1),
        'o_w_out': nrm(ks[20], (N_ODD, SGU_WIDTH, D_MODEL), SGU_WIDTH ** -0.5),
    }


def reference(x, norm_g, final_norm_g, e_w_in, e_shift_mu, e_w_decay_up, e_w0, e_a_up, e_a0,
              e_k_k, e_k_a, e_r_k, e_gn_g, e_gn_b, e_w_out, o_w_in, o_ln_g, o_ln_b, o_w_s,
              o_b_s, o_w_out):
    for layer in range(DEPTH):
        h = rmsnorm(x, norm_g[layer])
        i = layer // 2
        if layer % 2 == 0:
            x = x + even_layer(h, e_w_in[i], e_shift_mu[i], e_w_decay_up[i], e_w0[i], e_a_up[i],
                               e_a0[i], e_k_k[i], e_k_a[i], e_r_k[i], e_gn_g[i], e_gn_b[i], e_w_out[i])
        else:
            x = x + odd_layer(h, o_w_in[i], o_ln_g[i], o_ln_b[i], o_w_s[i], o_b_s[i], o_w_out[i])
    return rmsnorm(x, final_norm_g)
```

```python
import functools
import math

import jax
import jax.numpy as jnp
from jax import lax
from jax.experimental import pallas as pl
from jax.experimental.pallas import tpu as pltpu

F32 = jnp.float32
BF16 = jnp.bfloat16

D_MODEL = 4096
RWKV_WIDTH = 2048
RWKV_HEAD = 64
LORA = 128
SB_WIDTH = 2048
SB_HEAD = 128
SB_HEADS = SB_WIDTH // SB_HEAD
SB_BLOCK = 128
SGU_WIDTH = 4096
SGU_CHUNK = 128
SGU_GROUPS = 16
SGU_GROUP_DIM = SGU_WIDTH // SGU_GROUPS

RMS_EPS = 1e-6
GN_EPS = 64e-5
LN_EPS = 1e-5
L2_EPS = 1e-12

LANES = 128
RWKV_CHUNK = 64
RWKV_PAIRS_PER_STEP = 4
VMEM_LIMIT = 56 * 1024 * 1024

COL_R, COL_K, COL_V, COL_GA = 0, 2048, 4096, 6144
COL_Q, COL_SK, COL_SV, COL_GB = 8192, 10240, 12288, 14336
COL_WLO, COL_ALO = 16384, 16512
EVEN_COLS = 16640


def _params(sem):
    return pltpu.CompilerParams(dimension_semantics=sem, vmem_limit_bytes=VMEM_LIMIT)


def _dot(a, b):
    return jnp.dot(a.astype(BF16), b.astype(BF16), preferred_element_type=F32)


def _dot_nt(a, b):
    return lax.dot_general(a.astype(BF16), b.astype(BF16), (((1,), (1,)), ((), ())),
                           preferred_element_type=F32)


def _dot_tn(a, b):
    return lax.dot_general(a.astype(BF16), b.astype(BF16), (((0,), (0,)), ((), ())),
                           preferred_element_type=F32)


def _split(x):
    hi = x.astype(BF16)
    lo = (x - hi.astype(F32)).astype(BF16)
    return hi, lo


def _dot_exact_rhs(x, m):
    hi, lo = _split(x)
    return (jnp.dot(hi, m, preferred_element_type=F32)
            + jnp.dot(lo, m, preferred_element_type=F32))


def _dot_exact_lhs(m, x):
    hi, lo = _split(x)
    return (jnp.dot(m, hi, preferred_element_type=F32)
            + jnp.dot(m, lo, preferred_element_type=F32))


def _sigmoid(x):
    return 1.0 / (1.0 + jnp.exp(-x))


def _silu(x):
    return x * _sigmoid(x)


def _gelu(x):
    return 0.5 * x * (1.0 + lax.erf(x * (2.0 ** -0.5)))


def _rmsnorm_kernel(x_ref, g_ref, o_ref):
    x = x_ref[...]
    ms = jnp.mean(x * x, axis=-1, keepdims=True)
    o_ref[...] = (x * lax.rsqrt(ms + RMS_EPS) * g_ref[...]).astype(o_ref.dtype)


def _rmsnorm(x2, g, out_dtype):
    n, d = x2.shape
    tm = min(256, n)
    return pl.pallas_call(
        _rmsnorm_kernel,
        grid=(n // tm,),
        in_specs=[pl.BlockSpec((tm, d), lambda i: (i, 0)),
                  pl.BlockSpec((1, d), lambda i: (0, 0))],
        out_specs=pl.BlockSpec((tm, d), lambda i: (i, 0)),
        out_shape=jax.ShapeDtypeStruct((n, d), out_dtype),
        compiler_params=_params(("arbitrary",)),
        name="rmsnorm",
    )(x2, g.reshape(1, d))


def _mm_kernel(a_ref, w_ref, o_ref):
    o_ref[...] = jnp.dot(a_ref[...], w_ref[...], preferred_element_type=F32).astype(o_ref.dtype)


def _mm_act_kernel(a_ref, w_ref, o_ref, *, n_gelu):
    acc = jnp.dot(a_ref[...], w_ref[...], preferred_element_type=F32)
    j = pl.program_id(1)

    @pl.when(j < n_gelu)
    def _():
        o_ref[...] = _gelu(acc).astype(o_ref.dtype)

    @pl.when(j >= n_gelu)
    def _():
        o_ref[...] = _silu(acc).astype(o_ref.dtype)


def _matmul(a, w, tn, out_dtype, body, name):
    n, k = a.shape
    cols = w.shape[1]
    tm = min(1024, n)
    return pl.pallas_call(
        body,
        grid=(n // tm, cols // tn),
        in_specs=[pl.BlockSpec((tm, k), lambda i, j: (i, 0)),
                  pl.BlockSpec((k, tn), lambda i, j: (0, j))],
        out_specs=pl.BlockSpec((tm, tn), lambda i, j: (i, j)),
        out_shape=jax.ShapeDtypeStruct((n, cols), out_dtype),
        compiler_params=_params(("arbitrary", "arbitrary")),
        name=name,
    )(a, w)


def _mm_res2_kernel(a_ref, b_ref, wa_ref, wb_ref, x_ref, o_ref):
    acc = jnp.dot(a_ref[...], wa_ref[...], preferred_element_type=F32)
    acc = acc + jnp.dot(b_ref[...], wb_ref[...], preferred_element_type=F32)
    o_ref[...] = x_ref[...] + acc


def _matmul_res2(a, b, wa, wb, x2):
    n, k = a.shape
    cols = wa.shape[1]
    tm, tn = min(1024, n), 1024
    return pl.pallas_call(
        _mm_res2_kernel,
        grid=(n // tm, cols // tn),
        in_specs=[pl.BlockSpec((tm, k), lambda i, j: (i, 0)),
                  pl.BlockSpec((tm, k), lambda i, j: (i, 0)),
                  pl.BlockSpec((k, tn), lambda i, j: (0, j)),
                  pl.BlockSpec((k, tn), lambda i, j: (0, j)),
                  pl.BlockSpec((tm, tn), lambda i, j: (i, j))],
        out_specs=pl.BlockSpec((tm, tn), lambda i, j: (i, j)),
        out_shape=jax.ShapeDtypeStruct((n, cols), F32),
        compiler_params=_params(("arbitrary", "arbitrary")),
        name="even_out_proj",
    )(a, b, wa, wb, x2)


def _mm_res_kernel(a_ref, w_ref, x_ref, o_ref):
    o_ref[...] = x_ref[...] + jnp.dot(a_ref[...], w_ref[...], preferred_element_type=F32)


def _matmul_res(a, w, x2):
    n, k = a.shape
    cols = w.shape[1]
    tm, tn = min(1024, n), 1024
    return pl.pallas_call(
        _mm_res_kernel,
        grid=(n // tm, cols // tn),
        in_specs=[pl.BlockSpec((tm, k), lambda i, j: (i, 0)),
                  pl.BlockSpec((k, tn), lambda i, j: (0, j)),
                  pl.BlockSpec((tm, tn), lambda i, j: (i, j))],
        out_specs=pl.BlockSpec((tm, tn), lambda i, j: (i, j)),
        out_shape=jax.ShapeDtypeStruct((n, cols), F32),
        compiler_params=_params(("arbitrary", "arbitrary")),
        name="odd_out_proj",
    )(a, w, x2)


def _rwkv_kernel(pr_ref, pk_ref, pv_ref, pw_ref, pa_ref, g_ref,
                 mur_ref, muk_ref, muv_ref, muw_ref, mua_ref,
                 wup_ref, aup_ref, w0_ref, a0_ref, kk_ref, ka_ref, rk_ref,
                 gng_ref, gnb_ref,
                 o_ref,
                 s_ref, lr_ref, lk_ref, lv_ref, lw_ref, la_ref, *, chunk, pairs):
    L = chunk
    n = 2 * L
    c = pl.program_id(2)

    @pl.when(c == 0)
    def _():
        s_ref[...] = jnp.zeros_like(s_ref)
        lr_ref[...] = jnp.zeros_like(lr_ref)
        lk_ref[...] = jnp.zeros_like(lk_ref)
        lv_ref[...] = jnp.zeros_like(lv_ref)
        lw_ref[...] = jnp.zeros_like(lw_ref)
        la_ref[...] = jnp.zeros_like(la_ref)

    row = lax.broadcasted_iota(jnp.int32, (L, 1), 0)

    def shift_lerp(x, last, mu):
        prev = jnp.where(row == 0, last, pltpu.roll(x, 1, 0))
        return x + mu * (prev - x)

    ti = lax.broadcasted_iota(jnp.int32, (L, L), 0)
    tj = lax.broadcasted_iota(jnp.int32, (L, L), 1)
    tril_t = jnp.where(tj <= ti, 1.0, 0.0).astype(BF16)
    ei = lax.broadcasted_iota(jnp.int32, (LANES, LANES), 0)
    ej = lax.broadcasted_iota(jnp.int32, (LANES, LANES), 1)
    seg = jnp.where((ei // RWKV_HEAD) == (ej // RWKV_HEAD), 1.0, 0.0).astype(BF16)
    hmask = jnp.where((ei // L) == (ej // RWKV_HEAD), 1.0, 0.0)
    strict = ej < ei
    incl = ej <= ei
    eye = jnp.where(ei == ej, 1.0, 0.0)

    def stack(x):
        return jnp.concatenate([x, x], axis=0) * hmask

    wlo_raw = pw_ref[0]
    alo_raw = pa_ref[0]
    wlo = shift_lerp(wlo_raw, lw_ref[...], muw_ref[...])
    alo = shift_lerp(alo_raw, la_ref[...], mua_ref[...])
    lw_ref[...] = wlo_raw[L - 1:L, :]
    la_ref[...] = alo_raw[L - 1:L, :]
    tanh_wlo = jnp.tanh(wlo).astype(BF16)
    alo_b = alo.astype(BF16)
    exp_m05 = math.exp(-0.5)

    for i in range(pairs):
        sl = slice(i * LANES, (i + 1) * LANES)
        r_raw = pr_ref[0, :, sl]
        k_raw = pk_ref[0, :, sl]
        v_raw = pv_ref[0, :, sl]
        r = shift_lerp(r_raw, lr_ref[:, sl], mur_ref[:, sl])
        k = shift_lerp(k_raw, lk_ref[:, sl], muk_ref[:, sl])
        v = shift_lerp(v_raw, lv_ref[:, sl], muv_ref[:, sl])
        lr_ref[:, sl] = r_raw[L - 1:L, :]
        lk_ref[:, sl] = k_raw[L - 1:L, :]
        lv_ref[:, sl] = v_raw[L - 1:L, :]

        wpre = w0_ref[:, sl] + jnp.dot(tanh_wlo, wup_ref[:, sl], preferred_element_type=F32)
        ld = -exp_m05 * _sigmoid(wpre)
        a = _sigmoid(a0_ref[:, sl] + jnp.dot(alo_b, aup_ref[:, sl], preferred_element_type=F32))

        kk = k * kk_ref[:, sl]
        nrm = jnp.sqrt(_dot_exact_rhs(kk * kk, seg))
        kk = kk / jnp.maximum(nrm, L2_EPS)
        k2 = k * (1.0 + (a - 1.0) * ka_ref[:, sl])

        cum = _dot_exact_lhs(tril_t, ld)
        cum_last = cum[L - 1:L, :]
        w_incl = jnp.exp(cum)
        w_excl = jnp.exp(cum - ld)
        w_inv = jnp.exp(-cum)
        w_rest = jnp.exp(cum_last - cum)
        w_all = jnp.exp(cum_last)
        kb = kk * a

        a_s = stack(-kk * w_excl)
        r_s = stack(r * w_incl)
        b_s = stack(kb * w_inv)
        k_s = stack(k2 * w_inv)
        v_s = stack(v)
        bw_s = stack(kb * w_rest)
        kw_s = stack(k2 * w_rest)

        lhs = jnp.concatenate([a_s, r_s], axis=0).astype(BF16)
        rhs = jnp.concatenate([b_s, k_s], axis=0).astype(BF16)
        gram = _dot_nt(lhs, rhs)
        a_ab = jnp.where(strict, gram[:n, :n], 0.0)
        a_ak = jnp.where(strict, gram[:n, n:], 0.0)
        a_rb = jnp.where(incl, gram[n:, :n], 0.0)
        a_rk = jnp.where(incl, gram[n:, n:], 0.0)

        pw = a_ab
        inv = eye + pw
        steps = L.bit_length() - 2
        for _ in range(steps):
            pw = _dot(pw, pw)
            inv = inv + _dot(inv, pw)

        s_old = s_ref[i]
        t1 = _dot_nt(lhs, s_old)
        u = _dot(inv, t1[:n] + _dot(a_ak, v_s))
        uv = jnp.concatenate([u, v_s], axis=0).astype(BF16)
        y_s = t1[n:] + _dot(jnp.concatenate([a_rb, a_rk], axis=1), uv)
        y = y_s[:L] + y_s[L:]
        s_ref[i] = s_old * w_all + _dot_tn(uv, jnp.concatenate([bw_s, kw_s], axis=0))

        mean = _dot_exact_rhs(y, seg) * (1.0 / RWKV_HEAD)
        yc = y - mean
        var = _dot_exact_rhs(yc * yc, seg) * (1.0 / RWKV_HEAD)
        yn = yc * lax.rsqrt(var + GN_EPS) * gng_ref[:, sl] + gnb_ref[:, sl]
        bonus = _dot_exact_rhs(r * k2 * rk_ref[:, sl], seg) * v
        o_ref[0, :, sl] = ((yn + bonus) * _silu(g_ref[0, :, sl])).astype(o_ref.dtype)


def _rwkv(p3, mu, w_dec_up, w0, a_up, a0, k_k, k_a, r_k, gn_g, gn_b):
    b, s, _ = p3.shape
    L, pairs = RWKV_CHUNK, RWKV_PAIRS_PER_STEP
    cw = pairs * LANES
    row = lambda t: t.reshape(1, -1)

    def cols(off):
        return pl.BlockSpec((1, L, cw), lambda bi, gi, ci: (bi, ci, off // cw + gi))

    def lora_cols(off):
        return pl.BlockSpec((1, L, LORA), lambda bi, gi, ci: (bi, ci, off // LORA))

    vec = pl.BlockSpec((1, cw), lambda bi, gi, ci: (0, gi))
    vec_lora = pl.BlockSpec((1, LORA), lambda bi, gi, ci: (0, 0))
    up = pl.BlockSpec((LORA, cw), lambda bi, gi, ci: (0, gi))
    kern = functools.partial(_rwkv_kernel, chunk=L, pairs=pairs)
    return pl.pallas_call(
        kern,
        grid=(b, RWKV_WIDTH // cw, s // L),
        in_specs=[cols(COL_R), cols(COL_K), cols(COL_V), lora_cols(COL_WLO), lora_cols(COL_ALO),
                  cols(COL_GA),
                  vec, vec, vec, vec_lora, vec_lora,
                  up, up, vec, vec, vec, vec, vec, vec, vec],
        out_specs=pl.BlockSpec((1, L, cw), lambda bi, gi, ci: (bi, ci, gi)),
        out_shape=jax.ShapeDtypeStruct((b, s, RWKV_WIDTH), BF16),
        scratch_shapes=[pltpu.VMEM((pairs, LANES, LANES), F32),
                        pltpu.VMEM((1, cw), F32), pltpu.VMEM((1, cw), F32),
                        pltpu.VMEM((1, cw), F32),
                        pltpu.VMEM((1, LORA), F32), pltpu.VMEM((1, LORA), F32)],
        compiler_params=_params(("arbitrary", "arbitrary", "arbitrary")),
        name="rwkv7_mix",
    )(p3, p3, p3, p3, p3, p3,
      row(mu[0:2048]), row(mu[2048:4096]), row(mu[4096:6144]), row(mu[6144:6272]),
      row(mu[6272:6400]),
      w_dec_up.astype(BF16), a_up.astype(BF16), row(w0), row(a0), row(k_k), row(k_a), row(r_k),
      row(gn_g), row(gn_b))


def _sb_kernel(q_ref, k_ref, v_ref, g_ref, o_ref, *, scale):
    qi = pl.program_id(2)
    blk = SB_BLOCK
    q = q_ref[0].astype(BF16)
    ri = lax.broadcasted_iota(jnp.int32, (blk, blk), 0)
    ci = lax.broadcasted_iota(jnp.int32, (blk, blk), 1)
    causal = ci < ri
    mext = jnp.concatenate([jnp.where(ri > ci, 1.0, 0.0), jnp.ones((blk, blk), F32)],
                           axis=1).astype(BF16)

    def tile(j, masked):
        start = pl.multiple_of(j * blk, blk)
        kj = k_ref[0, pl.ds(start, blk), :].astype(BF16)
        vj = v_ref[0, pl.ds(start, blk), :].astype(BF16)
        z = _dot_nt(q, kj) * scale
        lk = -(jnp.maximum(z, 0.0) + jnp.log1p(jnp.exp(-jnp.abs(z))))
        if masked:
            lk = jnp.where(causal, lk, 0.0)
        sums = _dot_exact_rhs(lk, mext)
        return z + lk + sums[:, :blk], sums[:, blk:], vj

    logit, carry, vj = tile(qi, True)
    att = jnp.where(causal, jnp.exp(logit), 0.0)
    acc = jnp.dot(att.astype(BF16), vj, preferred_element_type=F32)

    def body(it, st):
        acc, carry = st
        logit, total, vj = tile(qi - 1 - it, False)
        att = jnp.exp(logit + carry)
        return acc + jnp.dot(att.astype(BF16), vj, preferred_element_type=F32), carry + total

    acc, carry = lax.fori_loop(0, qi, body, (acc, carry))
    o_ref[0] = (acc * _silu(g_ref[0])).astype(o_ref.dtype)


def _stick_breaking(p3):
    b, s, _ = p3.shape
    blk = SB_BLOCK

    def qcols(off):
        return pl.BlockSpec((1, blk, SB_HEAD), lambda bi, hi, qi: (bi, qi, off // SB_HEAD + hi))

    def kvcols(off):
        return pl.BlockSpec((1, s, SB_HEAD), lambda bi, hi, qi: (bi, 0, off // SB_HEAD + hi))

    kern = functools.partial(_sb_kernel, scale=float(SB_HEAD) ** -0.5)
    return pl.pallas_call(
        kern,
        grid=(b, SB_HEADS, s // blk),
        in_specs=[qcols(COL_Q), kvcols(COL_SK), kvcols(COL_SV), qcols(COL_GB)],
        out_specs=pl.BlockSpec((1, blk, SB_HEAD), lambda bi, hi, qi: (bi, qi, hi)),
        out_shape=jax.ShapeDtypeStruct((b, s, SB_WIDTH), BF16),
        compiler_params=_params(("arbitrary", "arbitrary", "arbitrary")),
        name="stick_breaking",
    )(p3, p3, p3, p3)


def _sgu_kernel(u_ref, v_ref, g_ref, lng_ref, lnb_ref, ws_ref, bt_ref, o_ref):
    v = v_ref[...]
    mu = jnp.mean(v, axis=-1, keepdims=True)
    vc = v - mu
    var = jnp.mean(vc * vc, axis=-1, keepdims=True)
    vn = (vc * lax.rsqrt(var + LN_EPS) * lng_ref[...] + lnb_ref[...]).astype(BF16)
    ri = lax.broadcasted_iota(jnp.int32, (SGU_CHUNK, SGU_CHUNK), 0)
    ci = lax.broadcasted_iota(jnp.int32, (SGU_CHUNK, SGU_CHUNK), 1)
    causal = ci <= ri
    for g in range(SGU_GROUPS):
        sl = slice(g * SGU_GROUP_DIM, (g + 1) * SGU_GROUP_DIM)
        w = jnp.where(causal, ws_ref[g], 0.0).astype(BF16)
        mixed = jnp.dot(w, vn[:, sl], preferred_element_type=F32) + bt_ref[:, g:g + 1]
        o_ref[:, sl] = (u_ref[:, sl] * mixed * g_ref[:, sl]).astype(o_ref.dtype)


def _sgu(uvg, ln_g, ln_b, w_s, b_s):
    n = uvg.shape[0]
    t = SGU_CHUNK

    def part(idx):
        return pl.BlockSpec((t, SGU_WIDTH), lambda i: (i, idx))

    vec = pl.BlockSpec((1, SGU_WIDTH), lambda i: (0, 0))
    return pl.pallas_call(
        _sgu_kernel,
        grid=(n // t,),
        in_specs=[part(0), part(1), part(2), vec, vec,
                  pl.BlockSpec((SGU_GROUPS, t, t), lambda i: (0, 0, 0)),
                  pl.BlockSpec((t, SGU_GROUPS), lambda i: (0, 0))],
        out_specs=pl.BlockSpec((t, SGU_WIDTH), lambda i: (i, 0)),
        out_shape=jax.ShapeDtypeStruct((n, SGU_WIDTH), BF16),
        compiler_params=_params(("arbitrary",)),
        name="sgu",
    )(uvg, uvg, uvg, ln_g.reshape(1, -1), ln_b.reshape(1, -1), w_s, b_s.T)


def kernel(x, norm_g, final_norm_g, e_w_in, e_shift_mu, e_w_decay_up, e_w0, e_a_up, e_a0,
           e_k_k, e_k_a, e_r_k, e_gn_g, e_gn_b, e_w_out, o_w_in, o_ln_g, o_ln_b, o_w_s,
           o_b_s, o_w_out):
    b, s, d = x.shape
    n = b * s
    x2 = x.reshape(n, d)

    w_in = e_w_in[0]
    w_in = jnp.concatenate([w_in[:, :6144], w_in[:, 6400:], w_in[:, 6144:6400]], axis=1).astype(BF16)
    h = _rmsnorm(x2, norm_g[0], BF16)
    p = _matmul(h, w_in, 1280, F32, _mm_kernel, "even_in_proj")
    p3 = p.reshape(b, s, EVEN_COLS)
    ya = _rwkv(p3, e_shift_mu[0], e_w_decay_up[0], e_w0[0], e_a_up[0], e_a0[0], e_k_k[0],
               e_k_a[0], e_r_k[0], e_gn_g[0], e_gn_b[0])
    yb = _stick_breaking(p3)
    w_out = e_w_out[0].astype(BF16)
    x2 = _matmul_res2(ya.reshape(n, RWKV_WIDTH), yb.reshape(n, SB_WIDTH),
                      w_out[:RWKV_WIDTH], w_out[RWKV_WIDTH:], x2)

    h = _rmsnorm(x2, norm_g[1], BF16)
    act = functools.partial(_mm_act_kernel, n_gelu=2 * SGU_WIDTH // 1024)
    uvg = _matmul(h, o_w_in[0].astype(BF16), 1024, F32, act, "odd_in_proj")
    y = _sgu(uvg, o_ln_g[0], o_ln_b[0], o_w_s[0], o_b_s[0])
    x2 = _matmul_res(y, o_w_out[0].astype(BF16), x2)

    out = _rmsnorm(x2, final_norm_g, x.dtype)
    return out.reshape(b, s, d)
```

```python
import functools
import math

import jax
import jax.numpy as jnp
from jax import lax
from jax.experimental import pallas as pl
from jax.experimental.pallas import tpu as pltpu

F32 = jnp.float32
BF16 = jnp.bfloat16

D_MODEL = 4096
RWKV_WIDTH = 2048
RWKV_HEAD = 64
LORA = 128
SB_WIDTH = 2048
SB_HEAD = 128
SB_HEADS = SB_WIDTH // SB_HEAD
SB_BLOCK = 128
SGU_WIDTH = 4096
SGU_CHUNK = 128
SGU_GROUPS = 16
SGU_GROUP_DIM = SGU_WIDTH // SGU_GROUPS

RMS_EPS = 1e-6
GN_EPS = 64e-5
LN_EPS = 1e-5
L2_EPS = 1e-12

LANES = 128
RWKV_CHUNK = 64
RWKV_PAIRS_PER_STEP = 8
SB_TQ = 512
SB_TK = 256
VMEM_LIMIT = 56 * 1024 * 1024

COL_R, COL_K, COL_V, COL_GA, COL_WLO, COL_ALO = 0, 2048, 4096, 6144, 8192, 8320
EVEN_A_COLS = 8448
COL_Q, COL_SK, COL_SV, COL_GB = 0, 2048, 4096, 6144
EVEN_B_COLS = 8192


def _params(sem):
    return pltpu.CompilerParams(dimension_semantics=sem, vmem_limit_bytes=VMEM_LIMIT)


def _dot(a, b):
    return jnp.dot(a.astype(BF16), b.astype(BF16), preferred_element_type=F32)


def _dot_nt(a, b):
    return lax.dot_general(a.astype(BF16), b.astype(BF16), (((1,), (1,)), ((), ())),
                           preferred_element_type=F32)


def _split(x):
    hi = x.astype(BF16)
    lo = (x - hi.astype(F32)).astype(BF16)
    return hi, lo


def _dot_exact_rhs(x, m):
    hi, lo = _split(x)
    return jnp.dot(jnp.concatenate([hi, lo], axis=1), jnp.concatenate([m, m], axis=0),
                   preferred_element_type=F32)


def _dot_exact_lhs(m, x):
    hi, lo = _split(x)
    return jnp.dot(jnp.concatenate([m, m], axis=1), jnp.concatenate([hi, lo], axis=0),
                   preferred_element_type=F32)


def _sigmoid(x):
    return 1.0 / (1.0 + jnp.exp(-x))


def _silu(x):
    return x * _sigmoid(x)


def _gelu(x):
    return 0.5 * x * (1.0 + lax.erf(x * (2.0 ** -0.5)))


def _rmsnorm_kernel(x_ref, g_ref, o_ref):
    x = x_ref[...]
    ms = jnp.mean(x * x, axis=-1, keepdims=True)
    o_ref[...] = (x * lax.rsqrt(ms + RMS_EPS) * g_ref[...]).astype(o_ref.dtype)


def _rmsnorm(x2, g, out_dtype):
    n, d = x2.shape
    tm = min(256, n)
    return pl.pallas_call(
        _rmsnorm_kernel,
        grid=(n // tm,),
        in_specs=[pl.BlockSpec((tm, d), lambda i: (i, 0)),
                  pl.BlockSpec((1, d), lambda i: (0, 0))],
        out_specs=pl.BlockSpec((tm, d), lambda i: (i, 0)),
        out_shape=jax.ShapeDtypeStruct((n, d), out_dtype),
        compiler_params=_params(("arbitrary",)),
        name="rmsnorm",
    )(x2, g.reshape(1, d))


def _mm_kernel(a_ref, w_ref, o_ref):
    o_ref[...] = jnp.dot(a_ref[...], w_ref[...], preferred_element_type=F32).astype(o_ref.dtype)


def _mm_qscale_kernel(a_ref, w_ref, o_ref, *, n_q, scale):
    acc = jnp.dot(a_ref[...], w_ref[...], preferred_element_type=F32)
    factor = jnp.where(pl.program_id(1) < n_q, scale, 1.0)
    o_ref[...] = (acc * factor).astype(o_ref.dtype)


def _mm_act_kernel(a_ref, w_ref, o_ref, *, n_gelu):
    acc = jnp.dot(a_ref[...], w_ref[...], preferred_element_type=F32)
    j = pl.program_id(1)

    @pl.when(j < n_gelu)
    def _():
        o_ref[...] = _gelu(acc).astype(o_ref.dtype)

    @pl.when(j >= n_gelu)
    def _():
        o_ref[...] = _silu(acc).astype(o_ref.dtype)


def _matmul(a, w, tn, out_dtype, body, name):
    n, k = a.shape
    cols = w.shape[1]
    tm = min(1024, n)
    return pl.pallas_call(
        body,
        grid=(n // tm, cols // tn),
        in_specs=[pl.BlockSpec((tm, k), lambda i, j: (i, 0)),
                  pl.BlockSpec((k, tn), lambda i, j: (0, j))],
        out_specs=pl.BlockSpec((tm, tn), lambda i, j: (i, j)),
        out_shape=jax.ShapeDtypeStruct((n, cols), out_dtype),
        compiler_params=_params(("arbitrary", "arbitrary")),
        name=name,
    )(a, w)


def _mm_res2_kernel(a_ref, b_ref, wa_ref, wb_ref, x_ref, o_ref):
    acc = jnp.dot(a_ref[...], wa_ref[...], preferred_element_type=F32)
    acc = acc + jnp.dot(b_ref[...], wb_ref[...], preferred_element_type=F32)
    o_ref[...] = x_ref[...] + acc


def _matmul_res2(a, b, wa, wb, x2):
    n, k = a.shape
    cols = wa.shape[1]
    tm, tn = min(1024, n), 1024
    return pl.pallas_call(
        _mm_res2_kernel,
        grid=(n // tm, cols // tn),
        in_specs=[pl.BlockSpec((tm, k), lambda i, j: (i, 0)),
                  pl.BlockSpec((tm, k), lambda i, j: (i, 0)),
                  pl.BlockSpec((k, tn), lambda i, j: (0, j)),
                  pl.BlockSpec((k, tn), lambda i, j: (0, j)),
                  pl.BlockSpec((tm, tn), lambda i, j: (i, j))],
        out_specs=pl.BlockSpec((tm, tn), lambda i, j: (i, j)),
        out_shape=jax.ShapeDtypeStruct((n, cols), F32),
        compiler_params=_params(("arbitrary", "arbitrary")),
        name="even_out_proj",
    )(a, b, wa, wb, x2)


def _mm_res_kernel(a_ref, w_ref, x_ref, o_ref):
    o_ref[...] = x_ref[...] + jnp.dot(a_ref[...], w_ref[...], preferred_element_type=F32)


def _matmul_res(a, w, x2):
    n, k = a.shape
    cols = w.shape[1]
    tm, tn = min(1024, n), 1024
    return pl.pallas_call(
        _mm_res_kernel,
        grid=(n // tm, cols // tn),
        in_specs=[pl.BlockSpec((tm, k), lambda i, j: (i, 0)),
                  pl.BlockSpec((k, tn), lambda i, j: (0, j)),
                  pl.BlockSpec((tm, tn), lambda i, j: (i, j))],
        out_specs=pl.BlockSpec((tm, tn), lambda i, j: (i, j)),
        out_shape=jax.ShapeDtypeStruct((n, cols), F32),
        compiler_params=_params(("arbitrary", "arbitrary")),
        name="odd_out_proj",
    )(a, w, x2)


def _bmm(a, b):
    return jnp.einsum("gij,gjk->gik", a.astype(BF16), b.astype(BF16), preferred_element_type=F32)


def _bmm_nt(a, b):
    return jnp.einsum("gik,gjk->gij", a.astype(BF16), b.astype(BF16), preferred_element_type=F32)


def _bmm_tn(a, b):
    return jnp.einsum("gki,gkj->gij", a.astype(BF16), b.astype(BF16), preferred_element_type=F32)


def _rwkv_kernel(pr_ref, pk_ref, pv_ref, pw_ref, pa_ref, g_ref,
                 mur_ref, muk_ref, muv_ref, muw_ref, mua_ref,
                 wup_ref, aup_ref, w0_ref, a0_ref, kk_ref, ka_ref, rk_ref,
                 gng_ref, gnb_ref,
                 o_ref,
                 s_ref, lr_ref, lk_ref, lv_ref, lw_ref, la_ref, *, chunk, pairs):
    L = chunk
    n = 2 * L
    c = pl.program_id(2)

    @pl.when(c == 0)
    def _():
        s_ref[...] = jnp.zeros_like(s_ref)
        lr_ref[...] = jnp.zeros_like(lr_ref)
        lk_ref[...] = jnp.zeros_like(lk_ref)
        lv_ref[...] = jnp.zeros_like(lv_ref)
        lw_ref[...] = jnp.zeros_like(lw_ref)
        la_ref[...] = jnp.zeros_like(la_ref)

    row = lax.broadcasted_iota(jnp.int32, (L, 1), 0)

    def shift_lerp(x_ref, last_ref, mu_ref):
        x = x_ref[0]
        prev = jnp.where(row == 0, last_ref[...], pltpu.roll(x, 1, 0))
        last_ref[...] = x[L - 1:L, :]
        return x + mu_ref[...] * (prev - x)

    ti = lax.broadcasted_iota(jnp.int32, (L, L), 0)
    tj = lax.broadcasted_iota(jnp.int32, (L, L), 1)
    tril_t = jnp.where(tj <= ti, 1.0, 0.0).astype(BF16)
    ei = lax.broadcasted_iota(jnp.int32, (LANES, LANES), 0)
    ej = lax.broadcasted_iota(jnp.int32, (LANES, LANES), 1)
    seg = jnp.where((ei // RWKV_HEAD) == (ej // RWKV_HEAD), 1.0, 0.0).astype(BF16)
    hmask = jnp.where((ei // L) == (ej // RWKV_HEAD), 1.0, 0.0)
    strict = ej < ei
    incl = ej <= ei
    eye = jnp.where(ei == ej, 1.0, 0.0)

    def lane_tiles(x):
        return [x[:, i * LANES:(i + 1) * LANES] for i in range(pairs)]

    def head_sum(x):
        rows = jnp.concatenate(lane_tiles(x), axis=0)
        sums = _dot_exact_rhs(rows, seg)
        return jnp.concatenate([sums[i * L:(i + 1) * L] for i in range(pairs)], axis=1)

    def stack(x):
        xs = jnp.stack(lane_tiles(x), axis=0)
        return jnp.concatenate([xs, xs], axis=1) * hmask

    r = shift_lerp(pr_ref, lr_ref, mur_ref)
    k = shift_lerp(pk_ref, lk_ref, muk_ref)
    v = shift_lerp(pv_ref, lv_ref, muv_ref)
    wlo = shift_lerp(pw_ref, lw_ref, muw_ref)
    alo = shift_lerp(pa_ref, la_ref, mua_ref)

    wpre = w0_ref[...] + _dot(jnp.tanh(wlo), wup_ref[...])
    ld = -math.exp(-0.5) * _sigmoid(wpre)
    a = _sigmoid(a0_ref[...] + _dot(alo, aup_ref[...]))

    kk = k * kk_ref[...]
    kk = kk / jnp.maximum(jnp.sqrt(head_sum(kk * kk)), L2_EPS)
    k2 = k * (1.0 + (a - 1.0) * ka_ref[...])

    cum = _dot_exact_lhs(tril_t, ld)
    cum_last = cum[L - 1:L, :]
    w_incl = jnp.exp(cum)
    w_excl = jnp.exp(cum - ld)
    w_inv = jnp.exp(-cum)
    w_rest = jnp.exp(cum_last - cum)
    w_all = jnp.stack(lane_tiles(jnp.exp(cum_last)), axis=0)
    kb = kk * a

    a_s = stack(-kk * w_excl)
    r_s = stack(r * w_incl)
    b_s = stack(kb * w_inv)
    k_s = stack(k2 * w_inv)
    v_s = stack(v)
    bw_s = stack(kb * w_rest)
    kw_s = stack(k2 * w_rest)

    lhs = jnp.concatenate([a_s, r_s], axis=1).astype(BF16)
    rhs = jnp.concatenate([b_s, k_s], axis=1).astype(BF16)
    gram = _bmm_nt(lhs, rhs)
    a_ab = jnp.where(strict, gram[:, :n, :n], 0.0)
    a_ak = jnp.where(strict, gram[:, :n, n:], 0.0)
    a_rb = jnp.where(incl, gram[:, n:, :n], 0.0)
    a_rk = jnp.where(incl, gram[:, n:, n:], 0.0)

    pw = a_ab
    inv = eye + pw
    for _ in range(L.bit_length() - 2):
        pw = _bmm(pw, pw)
        inv = inv + _bmm(inv, pw)

    s_old = s_ref[...]
    t1 = _bmm_nt(lhs, s_old)
    u = _bmm(inv, t1[:, :n] + _bmm(a_ak, v_s))
    uv = jnp.concatenate([u, v_s], axis=1).astype(BF16)
    y_s = t1[:, n:] + _bmm(jnp.concatenate([a_rb, a_rk], axis=2), uv)
    s_ref[...] = s_old * w_all + _bmm_tn(uv, jnp.concatenate([bw_s, kw_s], axis=1))
    y3 = y_s[:, :L] + y_s[:, L:]
    y = jnp.concatenate([y3[i] for i in range(pairs)], axis=1)

    yc = y - head_sum(y) * (1.0 / RWKV_HEAD)
    var = head_sum(yc * yc) * (1.0 / RWKV_HEAD)
    yn = yc * lax.rsqrt(var + GN_EPS) * gng_ref[...] + gnb_ref[...]
    bonus = head_sum(r * k2 * rk_ref[...]) * v
    o_ref[0] = ((yn + bonus) * _silu(g_ref[0])).astype(o_ref.dtype)


def _rwkv(p3, mu, w_dec_up, w0, a_up, a0, k_k, k_a, r_k, gn_g, gn_b):
    b, s, _ = p3.shape
    L, pairs = RWKV_CHUNK, RWKV_PAIRS_PER_STEP
    assert 2 * L == LANES
    cw = pairs * LANES
    row = lambda t: t.reshape(1, -1)

    def cols(off):
        return pl.BlockSpec((1, L, cw), lambda bi, gi, ci: (bi, ci, off // cw + gi))

    def lora_cols(off):
        return pl.BlockSpec((1, L, LORA), lambda bi, gi, ci: (bi, ci, off // LORA))

    vec = pl.BlockSpec((1, cw), lambda bi, gi, ci: (0, gi))
    vec_lora = pl.BlockSpec((1, LORA), lambda bi, gi, ci: (0, 0))
    up = pl.BlockSpec((LORA, cw), lambda bi, gi, ci: (0, gi))
    kern = functools.partial(_rwkv_kernel, chunk=L, pairs=pairs)
    return pl.pallas_call(
        kern,
        grid=(b, RWKV_WIDTH // cw, s // L),
        in_specs=[cols(COL_R), cols(COL_K), cols(COL_V), lora_cols(COL_WLO), lora_cols(COL_ALO),
                  cols(COL_GA),
                  vec, vec, vec, vec_lora, vec_lora,
                  up, up, vec, vec, vec, vec, vec, vec, vec],
        out_specs=pl.BlockSpec((1, L, cw), lambda bi, gi, ci: (bi, ci, gi)),
        out_shape=jax.ShapeDtypeStruct((b, s, RWKV_WIDTH), BF16),
        scratch_shapes=[pltpu.VMEM((pairs, LANES, LANES), F32),
                        pltpu.VMEM((1, cw), F32), pltpu.VMEM((1, cw), F32),
                        pltpu.VMEM((1, cw), F32),
                        pltpu.VMEM((1, LORA), F32), pltpu.VMEM((1, LORA), F32)],
        compiler_params=_params(("arbitrary", "arbitrary", "arbitrary")),
        name="rwkv7_mix",
    )(p3, p3, p3, p3, p3, p3,
      row(mu[0:2048]), row(mu[2048:4096]), row(mu[4096:6144]), row(mu[6144:6272]),
      row(mu[6272:6400]),
      w_dec_up.astype(BF16), a_up.astype(BF16), row(w0), row(a0), row(k_k), row(k_a), row(r_k),
      row(gn_g), row(gn_b))


def _sb_kernel(q_ref, k_ref, v_ref, g_ref, o_ref, *, tq, tk):
    blk = SB_BLOCK
    qs = pl.program_id(2)
    q0 = qs * tq
    q = q_ref[0]
    ri = lax.broadcasted_iota(jnp.int32, (blk, blk), 0)
    ci = lax.broadcasted_iota(jnp.int32, (blk, blk), 1)
    mext = jnp.concatenate([jnp.where(ri > ci, 1.0, 0.0), jnp.ones((blk, blk), F32)],
                           axis=1).astype(BF16)
    trow = lax.broadcasted_iota(jnp.int32, (tq, tk), 0)
    scol = lax.broadcasted_iota(jnp.int32, (tq, tk), 1)

    def tile(k0, carry, masked):
        kj = k_ref[0, pl.ds(k0, tk), :]
        vj = v_ref[0, pl.ds(k0, tk), :]
        z = _dot_nt(q, kj)
        lq = jnp.minimum(z, 0.0) - jnp.log(1.0 + jnp.exp(-jnp.abs(z)))
        lk = lq - z
        if masked:
            causal = (k0 + scol) < (q0 + trow)
            lk = jnp.where(causal, lk, 0.0)
        sums = [_dot_exact_rhs(lk[:, i * blk:(i + 1) * blk], mext) for i in range(tk // blk)]
        later, after = [], carry
        for sm in reversed(sums):
            later.append(sm[:, :blk] + after)
            after = after + sm[:, blk:]
        att = jnp.exp(lq + jnp.concatenate(later[::-1], axis=1))
        if masked:
            att = jnp.where(causal, att, 0.0)
        return jnp.dot(att.astype(BF16), vj, preferred_element_type=F32), after

    acc = jnp.zeros((tq, SB_HEAD), F32)
    carry = jnp.zeros((tq, blk), F32)
    for d in range(tq // tk - 1, -1, -1):
        part, carry = tile(pl.multiple_of(q0 + d * tk, tk), carry, True)
        acc = acc + part

    def body(it, st):
        acc, carry = st
        part, carry = tile(pl.multiple_of(q0 - (it + 1) * tk, tk), carry, False)
        return acc + part, carry

    acc, carry = lax.fori_loop(0, qs * (tq // tk), body, (acc, carry))
    o_ref[0] = (acc * _silu(g_ref[0].astype(F32))).astype(o_ref.dtype)


def _stick_breaking(p3):
    b, s, _ = p3.shape
    tq = min(SB_TQ, s)
    tk = min(SB_TK, s)

    def qcols(off):
        return pl.BlockSpec((1, tq, SB_HEAD), lambda bi, hi, qi: (bi, qi, off // SB_HEAD + hi))

    def kvcols(off):
        return pl.BlockSpec((1, s, SB_HEAD), lambda bi, hi, qi: (bi, 0, off // SB_HEAD + hi))

    kern = functools.partial(_sb_kernel, tq=tq, tk=tk)
    return pl.pallas_call(
        kern,
        grid=(b, SB_HEADS, s // tq),
        in_specs=[qcols(COL_Q), kvcols(COL_SK), kvcols(COL_SV), qcols(COL_GB)],
        out_specs=pl.BlockSpec((1, tq, SB_HEAD), lambda bi, hi, qi: (bi, qi, hi)),
        out_shape=jax.ShapeDtypeStruct((b, s, SB_WIDTH), BF16),
        compiler_params=_params(("arbitrary", "arbitrary", "arbitrary")),
        name="stick_breaking",
    )(p3, p3, p3, p3)


def _sgu_kernel(u_ref, v_ref, g_ref, lng_ref, lnb_ref, ws_ref, bt_ref, o_ref):
    v = v_ref[...].astype(F32)
    mu = jnp.mean(v, axis=-1, keepdims=True)
    vc = v - mu
    var = jnp.mean(vc * vc, axis=-1, keepdims=True)
    vn = (vc * lax.rsqrt(var + LN_EPS) * lng_ref[...] + lnb_ref[...]).astype(BF16)
    ri = lax.broadcasted_iota(jnp.int32, (SGU_CHUNK, SGU_CHUNK), 0)
    ci = lax.broadcasted_iota(jnp.int32, (SGU_CHUNK, SGU_CHUNK), 1)
    causal = ci <= ri
    for g in range(SGU_GROUPS):
        sl = slice(g * SGU_GROUP_DIM, (g + 1) * SGU_GROUP_DIM)
        w = jnp.where(causal, ws_ref[g], 0.0).astype(BF16)
        mixed = jnp.dot(w, vn[:, sl], preferred_element_type=F32) + bt_ref[:, g:g + 1]
        gate = u_ref[:, sl].astype(F32) * g_ref[:, sl].astype(F32)
        o_ref[:, sl] = (gate * mixed).astype(o_ref.dtype)


def _sgu(uvg, ln_g, ln_b, w_s, b_s):
    n = uvg.shape[0]
    t = SGU_CHUNK

    def part(idx):
        return pl.BlockSpec((t, SGU_WIDTH), lambda i: (i, idx))

    vec = pl.BlockSpec((1, SGU_WIDTH), lambda i: (0, 0))
    return pl.pallas_call(
        _sgu_kernel,
        grid=(n // t,),
        in_specs=[part(0), part(1), part(2), vec, vec,
                  pl.BlockSpec((SGU_GROUPS, t, t), lambda i: (0, 0, 0)),
                  pl.BlockSpec((t, SGU_GROUPS), lambda i: (0, 0))],
        out_specs=pl.BlockSpec((t, SGU_WIDTH), lambda i: (i, 0)),
        out_shape=jax.ShapeDtypeStruct((n, SGU_WIDTH), BF16),
        compiler_params=_params(("arbitrary",)),
        name="sgu",
    )(uvg, uvg, uvg, ln_g.reshape(1, -1), ln_b.reshape(1, -1), w_s, b_s.T)


def kernel(x, norm_g, final_norm_g, e_w_in, e_shift_mu, e_w_decay_up, e_w0, e_a_up, e_a0,
           e_k_k, e_k_a, e_r_k, e_gn_g, e_gn_b, e_w_out, o_w_in, o_ln_g, o_ln_b, o_w_s,
           o_b_s, o_w_out):
    b, s, d = x.shape
    n = b * s
    x2 = x.reshape(n, d)

    w_in = e_w_in[0]
    w_in_a = jnp.concatenate([w_in[:, :6144], w_in[:, 6400:8448], w_in[:, 6144:6400]], axis=1)
    h = _rmsnorm(x2, norm_g[0], BF16)
    pa = _matmul(h, w_in_a.astype(BF16), 768, F32, _mm_kernel, "even_in_proj_a")
    qscale = functools.partial(_mm_qscale_kernel, n_q=SB_WIDTH // 1024, scale=float(SB_HEAD) ** -0.5)
    pb = _matmul(h, w_in[:, 8448:].astype(BF16), 1024, BF16, qscale, "even_in_proj_b")
    ya = _rwkv(pa.reshape(b, s, EVEN_A_COLS), e_shift_mu[0], e_w_decay_up[0], e_w0[0], e_a_up[0],
               e_a0[0], e_k_k[0], e_k_a[0], e_r_k[0], e_gn_g[0], e_gn_b[0])
    yb = _stick_breaking(pb.reshape(b, s, EVEN_B_COLS))
    w_out = e_w_out[0].astype(BF16)
    x2 = _matmul_res2(ya.reshape(n, RWKV_WIDTH), yb.reshape(n, SB_WIDTH),
                      w_out[:RWKV_WIDTH], w_out[RWKV_WIDTH:], x2)

    h = _rmsnorm(x2, norm_g[1], BF16)
    act = functools.partial(_mm_act_kernel, n_gelu=2 * SGU_WIDTH // 1024)
    uvg = _matmul(h, o_w_in[0].astype(BF16), 1024, BF16, act, "odd_in_proj")
    y = _sgu(uvg, o_ln_g[0], o_ln_b[0], o_w_s[0], o_b_s[0])
    x2 = _matmul_res(y, o_w_out[0].astype(BF16), x2)

    out = _rmsnorm(x2, final_norm_g, x.dtype)
    return out.reshape(b, s, d)
```

```python
import functools
import math

import jax
import jax.numpy as jnp
from jax import lax
from jax.experimental import pallas as pl
from jax.experimental.pallas import tpu as pltpu

F32 = jnp.float32
BF16 = jnp.bfloat16

D_MODEL = 4096
RWKV_WIDTH = 2048
RWKV_HEAD = 64
LORA = 128
SB_WIDTH = 2048
SB_HEAD = 128
SB_HEADS = SB_WIDTH // SB_HEAD
SB_BLOCK = 128
SGU_WIDTH = 4096
SGU_CHUNK = 128
SGU_GROUPS = 16
SGU_GROUP_DIM = SGU_WIDTH // SGU_GROUPS

RMS_EPS = 1e-6
GN_EPS = 64e-5
LN_EPS = 1e-5
L2_EPS = 1e-12

LANES = 128
RWKV_CHUNK = 64
RWKV_PAIRS_PER_STEP = 16
SB_TQ = 512
SB_TD = 256
SB_TK = 512
VMEM_LIMIT = 56 * 1024 * 1024

COL_R, COL_K, COL_V, COL_GA, COL_WLO, COL_ALO = 0, 2048, 4096, 6144, 8192, 8320
EVEN_A_COLS = 8448
COL_Q, COL_SK, COL_SV, COL_GB = 0, 2048, 4096, 6144
EVEN_B_COLS = 8192


def _params(sem):
    return pltpu.CompilerParams(dimension_semantics=sem, vmem_limit_bytes=VMEM_LIMIT)


def _dot(a, b):
    return jnp.dot(a.astype(BF16), b.astype(BF16), preferred_element_type=F32)


def _dot_nt(a, b):
    return lax.dot_general(a.astype(BF16), b.astype(BF16), (((1,), (1,)), ((), ())),
                           preferred_element_type=F32)


def _split(x):
    hi = x.astype(BF16)
    lo = (x - hi.astype(F32)).astype(BF16)
    return hi, lo


def _dot_exact_rhs(x, m):
    hi, lo = _split(x)
    return jnp.dot(jnp.concatenate([hi, lo], axis=1), jnp.concatenate([m, m], axis=0),
                   preferred_element_type=F32)


def _dot_exact_lhs(m, x):
    hi, lo = _split(x)
    return jnp.dot(jnp.concatenate([m, m], axis=1), jnp.concatenate([hi, lo], axis=0),
                   preferred_element_type=F32)


def _sigmoid(x):
    return 1.0 / (1.0 + jnp.exp(-x))


def _silu(x):
    return x * _sigmoid(x)


def _gelu(x):
    return 0.5 * x * (1.0 + lax.erf(x * (2.0 ** -0.5)))


def _rmsnorm_kernel(x_ref, g_ref, o_ref):
    x = x_ref[...]
    ms = jnp.mean(x * x, axis=-1, keepdims=True)
    o_ref[...] = (x * lax.rsqrt(ms + RMS_EPS) * g_ref[...]).astype(o_ref.dtype)


def _rmsnorm(x2, g, out_dtype):
    n, d = x2.shape
    tm = min(256, n)
    return pl.pallas_call(
        _rmsnorm_kernel,
        grid=(n // tm,),
        in_specs=[pl.BlockSpec((tm, d), lambda i: (i, 0)),
                  pl.BlockSpec((1, d), lambda i: (0, 0))],
        out_specs=pl.BlockSpec((tm, d), lambda i: (i, 0)),
        out_shape=jax.ShapeDtypeStruct((n, d), out_dtype),
        compiler_params=_params(("arbitrary",)),
        name="rmsnorm",
    )(x2, g.reshape(1, d))


def _mm_kernel(a_ref, w_ref, o_ref):
    o_ref[...] = jnp.dot(a_ref[...], w_ref[...], preferred_element_type=F32).astype(o_ref.dtype)


def _mm_qscale_kernel(a_ref, w_ref, o_ref, *, n_q, scale):
    acc = jnp.dot(a_ref[...], w_ref[...], preferred_element_type=F32)
    factor = jnp.where(pl.program_id(1) < n_q, scale, 1.0)
    o_ref[...] = (acc * factor).astype(o_ref.dtype)


def _mm_gelu_kernel(a_ref, w_ref, o_ref):
    acc = jnp.dot(a_ref[...], w_ref[...], preferred_element_type=F32)
    o_ref[...] = _gelu(acc).astype(o_ref.dtype)


def _mm_silu_kernel(a_ref, w_ref, o_ref):
    acc = jnp.dot(a_ref[...], w_ref[...], preferred_element_type=F32)
    o_ref[...] = _silu(acc).astype(o_ref.dtype)


def _matmul(a, w, tn, out_dtype, body, name):
    n, k = a.shape
    cols = w.shape[1]
    tm = min(1024, n)
    return pl.pallas_call(
        body,
        grid=(n // tm, cols // tn),
        in_specs=[pl.BlockSpec((tm, k), lambda i, j: (i, 0)),
                  pl.BlockSpec((k, tn), lambda i, j: (0, j))],
        out_specs=pl.BlockSpec((tm, tn), lambda i, j: (i, j)),
        out_shape=jax.ShapeDtypeStruct((n, cols), out_dtype),
        compiler_params=_params(("arbitrary", "arbitrary")),
        name=name,
    )(a, w)


def _mm_res2_kernel(a_ref, b_ref, wa_ref, wb_ref, x_ref, o_ref):
    acc = jnp.dot(a_ref[...], wa_ref[...], preferred_element_type=F32)
    acc = acc + jnp.dot(b_ref[...], wb_ref[...], preferred_element_type=F32)
    o_ref[...] = x_ref[...] + acc


def _matmul_res2(a, b, w, x2):
    n, k = a.shape
    cols = w.shape[1]
    tm, tn = min(1024, n), 1024
    return pl.pallas_call(
        _mm_res2_kernel,
        grid=(n // tm, cols // tn),
        in_specs=[pl.BlockSpec((tm, k), lambda i, j: (i, 0)),
                  pl.BlockSpec((tm, k), lambda i, j: (i, 0)),
                  pl.BlockSpec((k, tn), lambda i, j: (0, j)),
                  pl.BlockSpec((k, tn), lambda i, j: (1, j)),
                  pl.BlockSpec((tm, tn), lambda i, j: (i, j))],
        out_specs=pl.BlockSpec((tm, tn), lambda i, j: (i, j)),
        out_shape=jax.ShapeDtypeStruct((n, cols), F32),
        compiler_params=_params(("arbitrary", "arbitrary")),
        name="even_out_proj",
    )(a, b, w, w, x2)


def _mm_res_kernel(a_ref, w_ref, x_ref, o_ref):
    o_ref[...] = x_ref[...] + jnp.dot(a_ref[...], w_ref[...], preferred_element_type=F32)


def _matmul_res(a, w, x2):
    n, k = a.shape
    cols = w.shape[1]
    tm, tn = min(1024, n), 1024
    return pl.pallas_call(
        _mm_res_kernel,
        grid=(n // tm, cols // tn),
        in_specs=[pl.BlockSpec((tm, k), lambda i, j: (i, 0)),
                  pl.BlockSpec((k, tn), lambda i, j: (0, j)),
                  pl.BlockSpec((tm, tn), lambda i, j: (i, j))],
        out_specs=pl.BlockSpec((tm, tn), lambda i, j: (i, j)),
        out_shape=jax.ShapeDtypeStruct((n, cols), F32),
        compiler_params=_params(("arbitrary", "arbitrary")),
        name="odd_out_proj",
    )(a, w, x2)


def _bmm(a, b):
    return jnp.einsum("gij,gjk->gik", a.astype(BF16), b.astype(BF16), preferred_element_type=F32)


def _bmm_nt(a, b):
    return jnp.einsum("gik,gjk->gij", a.astype(BF16), b.astype(BF16), preferred_element_type=F32)


def _bmm_tn(a, b):
    return jnp.einsum("gki,gkj->gij", a.astype(BF16), b.astype(BF16), preferred_element_type=F32)


def _rwkv_kernel(pr_ref, pk_ref, pv_ref, pw_ref, pa_ref, g_ref,
                 mur_ref, muk_ref, muv_ref, muw_ref, mua_ref,
                 wup_ref, aup_ref, w0_ref, a0_ref, kk_ref, ka_ref, rk_ref,
                 gng_ref, gnb_ref,
                 o_ref,
                 s_ref, lr_ref, lk_ref, lv_ref, lw_ref, la_ref, *, chunk, pairs):
    L = chunk
    n = 2 * L
    c = pl.program_id(2)

    @pl.when(c == 0)
    def _():
        s_ref[...] = jnp.zeros_like(s_ref)
        lr_ref[...] = jnp.zeros_like(lr_ref)
        lk_ref[...] = jnp.zeros_like(lk_ref)
        lv_ref[...] = jnp.zeros_like(lv_ref)
        lw_ref[...] = jnp.zeros_like(lw_ref)
        la_ref[...] = jnp.zeros_like(la_ref)

    row = lax.broadcasted_iota(jnp.int32, (L, 1), 0)

    def shift_lerp(x_ref, last_ref, mu_ref):
        x = x_ref[0]
        prev = jnp.where(row == 0, last_ref[...], pltpu.roll(x, 1, 0))
        last_ref[...] = x[L - 1:L, :]
        return x + mu_ref[...] * (prev - x)

    ti = lax.broadcasted_iota(jnp.int32, (L, L), 0)
    tj = lax.broadcasted_iota(jnp.int32, (L, L), 1)
    tril_t = jnp.where(tj <= ti, 1.0, 0.0).astype(BF16)
    ei = lax.broadcasted_iota(jnp.int32, (LANES, LANES), 0)
    ej = lax.broadcasted_iota(jnp.int32, (LANES, LANES), 1)
    seg = jnp.where((ei // RWKV_HEAD) == (ej // RWKV_HEAD), 1.0, 0.0).astype(BF16)
    hmask = jnp.where((ei // L) == (ej // RWKV_HEAD), 1.0, 0.0)
    strict = ej < ei
    incl = ej <= ei
    eye = jnp.where(ei == ej, 1.0, 0.0)

    def lane_tiles(x):
        return [x[:, i * LANES:(i + 1) * LANES] for i in range(pairs)]

    def head_sum(x):
        rows = jnp.concatenate(lane_tiles(x), axis=0)
        sums = _dot_exact_rhs(rows, seg)
        return jnp.concatenate([sums[i * L:(i + 1) * L] for i in range(pairs)], axis=1)

    def stack(x):
        xs = jnp.stack(lane_tiles(x), axis=0)
        return jnp.concatenate([xs, xs], axis=1) * hmask

    r = shift_lerp(pr_ref, lr_ref, mur_ref)
    k = shift_lerp(pk_ref, lk_ref, muk_ref)
    v = shift_lerp(pv_ref, lv_ref, muv_ref)
    wlo = shift_lerp(pw_ref, lw_ref, muw_ref)
    alo = shift_lerp(pa_ref, la_ref, mua_ref)

    wpre = w0_ref[...] + _dot(jnp.tanh(wlo), wup_ref[...])
    ld = -math.exp(-0.5) * _sigmoid(wpre)
    a = _sigmoid(a0_ref[...] + _dot(alo, aup_ref[...]))

    kk = k * kk_ref[...]
    kk = kk / jnp.maximum(jnp.sqrt(head_sum(kk * kk)), L2_EPS)
    k2 = k * (1.0 + (a - 1.0) * ka_ref[...])

    cum = _dot_exact_lhs(tril_t, ld)
    cum_last = cum[L - 1:L, :]
    w_incl = jnp.exp(cum)
    w_excl = jnp.exp(cum - ld)
    w_inv = jnp.exp(-cum)
    w_rest = jnp.exp(cum_last - cum)
    w_all = jnp.stack(lane_tiles(jnp.exp(cum_last)), axis=0)
    kb = kk * a

    a_s = stack(-kk * w_excl)
    r_s = stack(r * w_incl)
    b_s = stack(kb * w_inv)
    k_s = stack(k2 * w_inv)
    v_s = stack(v)
    bw_s = stack(kb * w_rest)
    kw_s = stack(k2 * w_rest)

    lhs = jnp.concatenate([a_s, r_s], axis=1).astype(BF16)
    rhs = jnp.concatenate([b_s, k_s], axis=1).astype(BF16)
    gram = _bmm_nt(lhs, rhs)
    a_ab = jnp.where(strict, gram[:, :n, :n], 0.0)
    a_ak = jnp.where(strict, gram[:, :n, n:], 0.0)
    a_rb = jnp.where(incl, gram[:, n:, :n], 0.0)
    a_rk = jnp.where(incl, gram[:, n:, n:], 0.0)

    pw = a_ab
    inv = eye + pw
    for _ in range(L.bit_length() - 2):
        pw = _bmm(pw, pw)
        inv = inv + _bmm(inv, pw)

    s_old = s_ref[...]
    t1 = _bmm_nt(lhs, s_old)
    u = _bmm(inv, t1[:, :n] + _bmm(a_ak, v_s))
    uv = jnp.concatenate([u, v_s], axis=1).astype(BF16)
    y_s = t1[:, n:] + _bmm(jnp.concatenate([a_rb, a_rk], axis=2), uv)
    s_ref[...] = s_old * w_all + _bmm_tn(uv, jnp.concatenate([bw_s, kw_s], axis=1))
    y3 = y_s[:, :L] + y_s[:, L:]
    y = jnp.concatenate([y3[i] for i in range(pairs)], axis=1)

    yc = y - head_sum(y) * (1.0 / RWKV_HEAD)
    var = head_sum(yc * yc) * (1.0 / RWKV_HEAD)
    yn = yc * lax.rsqrt(var + GN_EPS) * gng_ref[...] + gnb_ref[...]
    bonus = head_sum(r * k2 * rk_ref[...]) * v
    o_ref[0] = ((yn + bonus) * _silu(g_ref[0])).astype(o_ref.dtype)


def _rwkv(p3, mu, w_dec_up, w0, a_up, a0, k_k, k_a, r_k, gn_g, gn_b):
    b, s, _ = p3.shape
    L, pairs = RWKV_CHUNK, RWKV_PAIRS_PER_STEP
    assert 2 * L == LANES
    cw = pairs * LANES
    row = lambda t: t.reshape(1, -1)

    def cols(off):
        return pl.BlockSpec((1, L, cw), lambda bi, gi, ci: (bi, ci, off // cw + gi))

    def lora_cols(off):
        return pl.BlockSpec((1, L, LORA), lambda bi, gi, ci: (bi, ci, off // LORA))

    vec = pl.BlockSpec((1, cw), lambda bi, gi, ci: (0, gi))
    vec_lora = pl.BlockSpec((1, LORA), lambda bi, gi, ci: (0, 0))
    up = pl.BlockSpec((LORA, cw), lambda bi, gi, ci: (0, gi))
    kern = functools.partial(_rwkv_kernel, chunk=L, pairs=pairs)
    return pl.pallas_call(
        kern,
        grid=(b, RWKV_WIDTH // cw, s // L),
        in_specs=[cols(COL_R), cols(COL_K), cols(COL_V), lora_cols(COL_WLO), lora_cols(COL_ALO),
                  cols(COL_GA),
                  vec, vec, vec, vec_lora, vec_lora,
                  up, up, vec, vec, vec, vec, vec, vec, vec],
        out_specs=pl.BlockSpec((1, L, cw), lambda bi, gi, ci: (bi, ci, gi)),
        out_shape=jax.ShapeDtypeStruct((b, s, RWKV_WIDTH), BF16),
        scratch_shapes=[pltpu.VMEM((pairs, LANES, LANES), F32),
                        pltpu.VMEM((1, cw), F32), pltpu.VMEM((1, cw), F32),
                        pltpu.VMEM((1, cw), F32),
                        pltpu.VMEM((1, LORA), F32), pltpu.VMEM((1, LORA), F32)],
        compiler_params=_params(("arbitrary", "arbitrary", "arbitrary")),
        name="rwkv7_mix",
    )(p3, p3, p3, p3, p3, p3,
      row(mu[0:2048]), row(mu[2048:4096]), row(mu[4096:6144]), row(mu[6144:6272]),
      row(mu[6272:6400]),
      w_dec_up.astype(BF16), a_up.astype(BF16), row(w0), row(a0), row(k_k), row(k_a), row(r_k),
      row(gn_g), row(gn_b))


def _sb_kernel(q_ref, k_ref, v_ref, g_ref, o_ref, *, tq, td, tk):
    blk = SB_BLOCK
    qs = pl.program_id(2)
    q0 = qs * tq
    ri = lax.broadcasted_iota(jnp.int32, (blk, blk), 0)
    ci = lax.broadcasted_iota(jnp.int32, (blk, blk), 1)
    mext = jnp.concatenate([jnp.where(ri > ci, 1.0, 0.0), jnp.ones((blk, blk), F32)],
                           axis=1).astype(BF16)
    causal = (lax.broadcasted_iota(jnp.int32, (td, td), 1)
              < lax.broadcasted_iota(jnp.int32, (td, td), 0))

    def tile(q, k0, width, carry, masked):
        kj = k_ref[0, pl.ds(k0, width), :]
        vj = v_ref[0, pl.ds(k0, width), :]
        z = _dot_nt(q, kj)
        lq = jnp.minimum(z, 0.0) - jnp.log(1.0 + jnp.exp(-jnp.abs(z)))
        lk = lq - z
        if masked:
            lk = jnp.where(causal, lk, 0.0)
        sums = [_dot_exact_rhs(lk[:, i * blk:(i + 1) * blk], mext) for i in range(width // blk)]
        later, after = [], carry
        for sm in reversed(sums):
            later.append(sm[:, :blk] + after)
            after = after + sm[:, blk:]
        att = jnp.exp(lq + jnp.concatenate(later[::-1], axis=1))
        if masked:
            att = jnp.where(causal, att, 0.0)
        return jnp.dot(att.astype(BF16), vj, preferred_element_type=F32), after

    accs, carries = [], []
    for rb in range(tq // td):
        q = q_ref[0, rb * td:(rb + 1) * td, :]
        acc = jnp.zeros((td, SB_HEAD), F32)
        carry = jnp.zeros((td, blk), F32)
        for kb in range(rb, -1, -1):
            part, carry = tile(q, pl.multiple_of(q0 + kb * td, td), td, carry, kb == rb)
            acc = acc + part
        accs.append(acc)
        carries.append(carry)
    acc = jnp.concatenate(accs, axis=0)
    carry = jnp.concatenate(carries, axis=0)

    q = q_ref[0]

    def body(it, st):
        acc, carry = st
        part, carry = tile(q, pl.multiple_of(q0 - (it + 1) * tk, tk), tk, carry, False)
        return acc + part, carry

    acc, carry = lax.fori_loop(0, qs * (tq // tk), body, (acc, carry))
    o_ref[0] = (acc * _silu(g_ref[0].astype(F32))).astype(o_ref.dtype)


def _stick_breaking(p3):
    b, s, _ = p3.shape
    tq = min(SB_TQ, s)
    td = min(SB_TD, s)
    tk = min(SB_TK, s)
    assert tq % tk == 0 and tq % td == 0

    def qcols(off):
        return pl.BlockSpec((1, tq, SB_HEAD), lambda bi, hi, qi: (bi, qi, off // SB_HEAD + hi))

    def kvcols(off):
        return pl.BlockSpec((1, s, SB_HEAD), lambda bi, hi, qi: (bi, 0, off // SB_HEAD + hi))

    kern = functools.partial(_sb_kernel, tq=tq, td=td, tk=tk)
    return pl.pallas_call(
        kern,
        grid=(b, SB_HEADS, s // tq),
        in_specs=[qcols(COL_Q), kvcols(COL_SK), kvcols(COL_SV), qcols(COL_GB)],
        out_specs=pl.BlockSpec((1, tq, SB_HEAD), lambda bi, hi, qi: (bi, qi, hi)),
        out_shape=jax.ShapeDtypeStruct((b, s, SB_WIDTH), BF16),
        compiler_params=_params(("arbitrary", "arbitrary", "arbitrary")),
        name="stick_breaking",
    )(p3, p3, p3, p3)


def _sgu_kernel(u_ref, v_ref, g_ref, lng_ref, lnb_ref, ws_ref, bt_ref, o_ref):
    v = v_ref[...].astype(F32)
    mu = jnp.mean(v, axis=-1, keepdims=True)
    vc = v - mu
    var = jnp.mean(vc * vc, axis=-1, keepdims=True)
    vn = (vc * lax.rsqrt(var + LN_EPS) * lng_ref[...] + lnb_ref[...]).astype(BF16)
    ri = lax.broadcasted_iota(jnp.int32, (SGU_CHUNK, SGU_CHUNK), 0)
    ci = lax.broadcasted_iota(jnp.int32, (SGU_CHUNK, SGU_CHUNK), 1)
    causal = ci <= ri
    for g in range(SGU_GROUPS):
        sl = slice(g * SGU_GROUP_DIM, (g + 1) * SGU_GROUP_DIM)
        w = jnp.where(causal, ws_ref[g], 0.0).astype(BF16)
        mixed = jnp.dot(w, vn[:, sl], preferred_element_type=F32) + bt_ref[:, g:g + 1]
        gate = u_ref[:, sl].astype(F32) * g_ref[:, sl].astype(F32)
        o_ref[:, sl] = (gate * mixed).astype(o_ref.dtype)


def _sgu(uv, g, ln_g, ln_b, w_s, b_s):
    n = uv.shape[0]
    t = SGU_CHUNK

    def part(idx):
        return pl.BlockSpec((t, SGU_WIDTH), lambda i: (i, idx))

    vec = pl.BlockSpec((1, SGU_WIDTH), lambda i: (0, 0))
    return pl.pallas_call(
        _sgu_kernel,
        grid=(n // t,),
        in_specs=[part(0), part(1), part(0), vec, vec,
                  pl.BlockSpec((SGU_GROUPS, t, t), lambda i: (0, 0, 0)),
                  pl.BlockSpec((t, SGU_GROUPS), lambda i: (0, 0))],
        out_specs=pl.BlockSpec((t, SGU_WIDTH), lambda i: (i, 0)),
        out_shape=jax.ShapeDtypeStruct((n, SGU_WIDTH), BF16),
        compiler_params=_params(("arbitrary",)),
        name="sgu",
    )(uv, uv, g, ln_g.reshape(1, -1), ln_b.reshape(1, -1), w_s, b_s.T)


def kernel(x, norm_g, final_norm_g, e_w_in, e_shift_mu, e_w_decay_up, e_w0, e_a_up, e_a0,
           e_k_k, e_k_a, e_r_k, e_gn_g, e_gn_b, e_w_out, o_w_in, o_ln_g, o_ln_b, o_w_s,
           o_b_s, o_w_out):
    b, s, d = x.shape
    n = b * s
    x2 = x.reshape(n, d)

    w_in = e_w_in[0]
    w_in_a = jnp.concatenate([w_in[:, :6144], w_in[:, 6400:8448], w_in[:, 6144:6400]], axis=1)
    h = _rmsnorm(x2, norm_g[0], BF16)
    pa = _matmul(h, w_in_a.astype(BF16), 768, F32, _mm_kernel, "even_in_proj_a")
    qscale = functools.partial(_mm_qscale_kernel, n_q=SB_WIDTH // 1024, scale=float(SB_HEAD) ** -0.5)
    pb = _matmul(h, w_in[:, 8448:].astype(BF16), 1024, BF16, qscale, "even_in_proj_b")
    ya = _rwkv(pa.reshape(b, s, EVEN_A_COLS), e_shift_mu[0], e_w_decay_up[0], e_w0[0], e_a_up[0],
               e_a0[0], e_k_k[0], e_k_a[0], e_r_k[0], e_gn_g[0], e_gn_b[0])
    yb = _stick_breaking(pb.reshape(b, s, EVEN_B_COLS))
    x2 = _matmul_res2(ya.reshape(n, RWKV_WIDTH), yb.reshape(n, SB_WIDTH),
                      e_w_out[0].astype(BF16), x2)

    h = _rmsnorm(x2, norm_g[1], BF16)
    w_in = o_w_in[0].astype(BF16)
    uv = _matmul(h, w_in[:, :2 * SGU_WIDTH], 1024, BF16, _mm_gelu_kernel, "odd_in_proj_uv")
    g = _matmul(h, w_in[:, 2 * SGU_WIDTH:], 1024, BF16, _mm_silu_kernel, "odd_in_proj_g")
    y = _sgu(uv, g, o_ln_g[0], o_ln_b[0], o_w_s[0], o_b_s[0])
    x2 = _matmul_res(y, o_w_out[0].astype(BF16), x2)

    out = _rmsnorm(x2, final_norm_g, x.dtype)
    return out.reshape(b, s, d)
```

```python
import functools
import math

import jax
import jax.numpy as jnp
from jax import lax
from jax.experimental import pallas as pl
from jax.experimental.pallas import tpu as pltpu

F32 = jnp.float32
BF16 = jnp.bfloat16

D_MODEL = 4096
RWKV_WIDTH = 2048
RWKV_HEAD = 64
LORA = 128
SB_WIDTH = 2048
SB_HEAD = 128
SB_HEADS = SB_WIDTH // SB_HEAD
SB_BLOCK = 128
SGU_WIDTH = 4096
SGU_CHUNK = 128
SGU_GROUPS = 16
SGU_GROUP_DIM = SGU_WIDTH // SGU_GROUPS

RMS_EPS = 1e-6
GN_EPS = 64e-5
LN_EPS = 1e-5
L2_EPS = 1e-12

LANES = 128
RWKV_CHUNK = 64
RWKV_PAIRS_PER_STEP = 16
SB_TQ = 512
SB_TD = 256
SB_TK = 512
OUT_TM = 512
OUT_TK = 512
OUT_TN = 1024
VMEM_LIMIT = 56 * 1024 * 1024

COL_R, COL_K, COL_V, COL_GA, COL_WLO, COL_ALO = 0, 2048, 4096, 6144, 8192, 8320
EVEN_A_COLS = 8448
COL_Q, COL_SK, COL_SV, COL_GB = 0, 2048, 4096, 6144
EVEN_B_COLS = 8192


def _params(sem):
    return pltpu.CompilerParams(dimension_semantics=sem, vmem_limit_bytes=VMEM_LIMIT)


def _dot(a, b):
    return jnp.dot(a.astype(BF16), b.astype(BF16), preferred_element_type=F32)


def _dot_nt(a, b):
    return lax.dot_general(a.astype(BF16), b.astype(BF16), (((1,), (1,)), ((), ())),
                           preferred_element_type=F32)


def _split(x):
    hi = x.astype(BF16)
    lo = (x - hi.astype(F32)).astype(BF16)
    return hi, lo


def _dot_exact_rhs(x, m):
    hi, lo = _split(x)
    return jnp.dot(jnp.concatenate([hi, lo], axis=1), jnp.concatenate([m, m], axis=0),
                   preferred_element_type=F32)


def _dot_exact_lhs(m, x):
    hi, lo = _split(x)
    return jnp.dot(jnp.concatenate([m, m], axis=1), jnp.concatenate([hi, lo], axis=0),
                   preferred_element_type=F32)


def _sigmoid(x):
    return 1.0 / (1.0 + jnp.exp(-x))


def _silu(x):
    return x * _sigmoid(x)


def _gelu(x):
    return 0.5 * x * (1.0 + lax.erf(x * (2.0 ** -0.5)))


def _rmsnorm_kernel(x_ref, g_ref, o_ref):
    x = x_ref[...]
    ms = jnp.mean(x * x, axis=-1, keepdims=True)
    o_ref[...] = (x * lax.rsqrt(ms + RMS_EPS) * g_ref[...]).astype(o_ref.dtype)


def _rmsnorm(x2, g, out_dtype):
    n, d = x2.shape
    tm = min(256, n)
    return pl.pallas_call(
        _rmsnorm_kernel,
        grid=(n // tm,),
        in_specs=[pl.BlockSpec((tm, d), lambda i: (i, 0)),
                  pl.BlockSpec((1, d), lambda i: (0, 0))],
        out_specs=pl.BlockSpec((tm, d), lambda i: (i, 0)),
        out_shape=jax.ShapeDtypeStruct((n, d), out_dtype),
        compiler_params=_params(("arbitrary",)),
        name="rmsnorm",
    )(x2, g.reshape(1, d))


def _mm_kernel(a_ref, w_ref, o_ref):
    o_ref[...] = jnp.dot(a_ref[...], w_ref[...], preferred_element_type=F32).astype(o_ref.dtype)


def _mm_qscale_kernel(a_ref, w_ref, o_ref, *, n_q, scale):
    acc = jnp.dot(a_ref[...], w_ref[...], preferred_element_type=F32)
    factor = jnp.where(pl.program_id(1) < n_q, scale, 1.0)
    o_ref[...] = (acc * factor).astype(o_ref.dtype)


def _mm_gelu_kernel(a_ref, w_ref, o_ref):
    acc = jnp.dot(a_ref[...], w_ref[...], preferred_element_type=F32)
    o_ref[...] = _gelu(acc).astype(o_ref.dtype)


def _mm_silu_kernel(a_ref, w_ref, o_ref):
    acc = jnp.dot(a_ref[...], w_ref[...], preferred_element_type=F32)
    o_ref[...] = _silu(acc).astype(o_ref.dtype)


def _matmul(a, w, tn, out_dtype, body, name, col0=0, cols=None):
    n, k = a.shape
    cols = w.shape[1] if cols is None else cols
    assert col0 % tn == 0 and cols % tn == 0
    tm = min(1024, n)
    j0 = col0 // tn
    return pl.pallas_call(
        body,
        grid=(n // tm, cols // tn),
        in_specs=[pl.BlockSpec((tm, k), lambda i, j: (i, 0)),
                  pl.BlockSpec((k, tn), lambda i, j: (0, j0 + j))],
        out_specs=pl.BlockSpec((tm, tn), lambda i, j: (i, j)),
        out_shape=jax.ShapeDtypeStruct((n, cols), out_dtype),
        compiler_params=_params(("arbitrary", "arbitrary")),
        name=name,
    )(a, w)


def _mm_res2_kernel(a_ref, b_ref, wa_ref, wb_ref, x_ref, o_ref):
    acc = jnp.dot(a_ref[...], wa_ref[...], preferred_element_type=F32)
    acc = acc + jnp.dot(b_ref[...], wb_ref[...], preferred_element_type=F32)
    o_ref[...] = x_ref[...] + acc


def _matmul_res2(a, b, w, x2):
    n, k = a.shape
    cols = w.shape[1]
    tm, tn = min(1024, n), 1024
    return pl.pallas_call(
        _mm_res2_kernel,
        grid=(n // tm, cols // tn),
        in_specs=[pl.BlockSpec((tm, k), lambda i, j: (i, 0)),
                  pl.BlockSpec((tm, k), lambda i, j: (i, 0)),
                  pl.BlockSpec((k, tn), lambda i, j: (0, j)),
                  pl.BlockSpec((k, tn), lambda i, j: (1, j)),
                  pl.BlockSpec((tm, tn), lambda i, j: (i, j))],
        out_specs=pl.BlockSpec((tm, tn), lambda i, j: (i, j)),
        out_shape=jax.ShapeDtypeStruct((n, cols), F32),
        compiler_params=_params(("arbitrary", "arbitrary")),
        name="even_out_proj",
    )(a, b, w, w, x2)


def _slabs(width):
    return [slice(c, c + OUT_TN) for c in range(0, width, OUT_TN)]


def _odd_tail_kernel(u_ref, v_ref, gate_ref, lng_ref, lnb_ref, ws_ref, bs_ref, w_ref, x_ref, fg_ref,
                     o_ref, mu_ref, rstd_ref, *, tk):
    k = pl.program_id(1)
    tm, d = o_ref.shape

    @pl.when(k == 0)
    def _():
        width = v_ref.shape[1]
        mu = sum(jnp.sum(v_ref[:, sl].astype(F32), axis=-1, keepdims=True)
                 for sl in _slabs(width)) * (1.0 / width)
        var = sum(jnp.sum(jnp.square(v_ref[:, sl].astype(F32) - mu), axis=-1, keepdims=True)
                  for sl in _slabs(width)) * (1.0 / width)
        mu_ref[...] = mu
        rstd_ref[...] = lax.rsqrt(var + LN_EPS)
        o_ref[...] = jnp.zeros_like(o_ref)

    vk = v_ref[:, pl.ds(pl.multiple_of(k * tk, tk), tk)].astype(F32)
    vn = ((vk - mu_ref[...]) * rstd_ref[...] * lng_ref[...] + lnb_ref[...]).astype(BF16)
    ri = lax.broadcasted_iota(jnp.int32, (SGU_CHUNK, SGU_CHUNK), 0)
    ci = lax.broadcasted_iota(jnp.int32, (SGU_CHUNK, SGU_CHUNK), 1)
    causal = ci <= ri
    cols = []
    for gg in range(tk // SGU_GROUP_DIM):
        grp = k * (tk // SGU_GROUP_DIM) + gg
        wmix = jnp.where(causal, ws_ref[grp], 0.0).astype(BF16)
        bias = bs_ref[grp]
        sl = slice(gg * SGU_GROUP_DIM, (gg + 1) * SGU_GROUP_DIM)
        rows = [jnp.dot(wmix, vn[c * SGU_CHUNK:(c + 1) * SGU_CHUNK, sl], preferred_element_type=F32) + bias
                for c in range(tm // SGU_CHUNK)]
        cols.append(jnp.concatenate(rows, axis=0))
    mixed = jnp.concatenate(cols, axis=1)
    y = (u_ref[...].astype(F32) * gate_ref[...].astype(F32) * mixed).astype(BF16)
    for sl in _slabs(d):
        o_ref[:, sl] += jnp.dot(y, w_ref[:, sl], preferred_element_type=F32)
    res_cols = pl.ds(pl.multiple_of(k * tk, tk), tk)
    o_ref[:, res_cols] += x_ref[...]

    @pl.when(k == pl.num_programs(1) - 1)
    def _():
        ss = sum(jnp.sum(jnp.square(o_ref[:, sl]), axis=-1, keepdims=True) for sl in _slabs(d))
        r = lax.rsqrt(ss * (1.0 / d) + RMS_EPS)
        for sl in _slabs(d):
            o_ref[:, sl] = o_ref[:, sl] * r * fg_ref[:, sl]


def _odd_tail(uv, gate, ln_g, ln_b, w_s, b_s, w, x2, fg):
    n, d = x2.shape
    tm, tk = min(OUT_TM, n), OUT_TK
    assert tm % SGU_CHUNK == 0 and tk % SGU_GROUP_DIM == 0 and d == SGU_WIDTH
    row = pl.BlockSpec((tm, d), lambda i, k: (i, 0))
    chunk = pl.BlockSpec((tm, tk), lambda i, k: (i, k))
    kvec = pl.BlockSpec((1, tk), lambda i, k: (0, k))
    return pl.pallas_call(
        functools.partial(_odd_tail_kernel, tk=tk),
        grid=(n // tm, SGU_WIDTH // tk),
        in_specs=[chunk,
                  pl.BlockSpec((tm, SGU_WIDTH), lambda i, k: (i, 1)),
                  chunk,
                  kvec, kvec,
                  pl.BlockSpec((SGU_GROUPS, SGU_CHUNK, SGU_CHUNK), lambda i, k: (0, 0, 0)),
                  pl.BlockSpec((SGU_GROUPS, SGU_CHUNK, 1), lambda i, k: (0, 0, 0)),
                  pl.BlockSpec((tk, d), lambda i, k: (k, 0)),
                  chunk,
                  pl.BlockSpec((1, d), lambda i, k: (0, 0))],
        out_specs=row,
        out_shape=jax.ShapeDtypeStruct((n, d), F32),
        scratch_shapes=[pltpu.VMEM((tm, 1), F32), pltpu.VMEM((tm, 1), F32)],
        compiler_params=_params(("arbitrary", "arbitrary")),
        name="odd_tail",
    )(uv, uv, gate, ln_g.reshape(1, -1), ln_b.reshape(1, -1), w_s, b_s[:, :, None], w, x2,
      fg.reshape(1, d))


def _bmm(a, b):
    return jnp.einsum("gij,gjk->gik", a.astype(BF16), b.astype(BF16), preferred_element_type=F32)


def _bmm_nt(a, b):
    return jnp.einsum("gik,gjk->gij", a.astype(BF16), b.astype(BF16), preferred_element_type=F32)


def _bmm_tn(a, b):
    return jnp.einsum("gki,gkj->gij", a.astype(BF16), b.astype(BF16), preferred_element_type=F32)


def _rwkv_kernel(pr_ref, pk_ref, pv_ref, pw_ref, pa_ref, g_ref,
                 mur_ref, muk_ref, muv_ref, muw_ref, mua_ref,
                 wup_ref, aup_ref, w0_ref, a0_ref, kk_ref, ka_ref, rk_ref,
                 gng_ref, gnb_ref,
                 o_ref,
                 s_ref, lr_ref, lk_ref, lv_ref, lw_ref, la_ref, *, chunk, pairs):
    L = chunk
    n = 2 * L
    c = pl.program_id(2)

    @pl.when(c == 0)
    def _():
        s_ref[...] = jnp.zeros_like(s_ref)
        lr_ref[...] = jnp.zeros_like(lr_ref)
        lk_ref[...] = jnp.zeros_like(lk_ref)
        lv_ref[...] = jnp.zeros_like(lv_ref)
        lw_ref[...] = jnp.zeros_like(lw_ref)
        la_ref[...] = jnp.zeros_like(la_ref)

    row = lax.broadcasted_iota(jnp.int32, (L, 1), 0)

    def shift_lerp(x_ref, last_ref, mu_ref):
        x = x_ref[0]
        prev = jnp.where(row == 0, last_ref[...], pltpu.roll(x, 1, 0))
        last_ref[...] = x[L - 1:L, :]
        return x + mu_ref[...] * (prev - x)

    ti = lax.broadcasted_iota(jnp.int32, (L, L), 0)
    tj = lax.broadcasted_iota(jnp.int32, (L, L), 1)
    tril_t = jnp.where(tj <= ti, 1.0, 0.0).astype(BF16)
    ei = lax.broadcasted_iota(jnp.int32, (LANES, LANES), 0)
    ej = lax.broadcasted_iota(jnp.int32, (LANES, LANES), 1)
    seg = jnp.where((ei // RWKV_HEAD) == (ej // RWKV_HEAD), 1.0, 0.0).astype(BF16)
    hmask = jnp.where((ei // L) == (ej // RWKV_HEAD), 1.0, 0.0)
    strict = ej < ei
    incl = ej <= ei
    eye = jnp.where(ei == ej, 1.0, 0.0)

    def lane_tiles(x):
        return [x[:, i * LANES:(i + 1) * LANES] for i in range(pairs)]

    def head_sum(x):
        rows = jnp.concatenate(lane_tiles(x), axis=0)
        sums = _dot_exact_rhs(rows, seg)
        return jnp.concatenate([sums[i * L:(i + 1) * L] for i in range(pairs)], axis=1)

    def stack(x):
        xs = jnp.stack(lane_tiles(x), axis=0)
        return jnp.concatenate([xs, xs], axis=1) * hmask

    r = shift_lerp(pr_ref, lr_ref, mur_ref)
    k = shift_lerp(pk_ref, lk_ref, muk_ref)
    v = shift_lerp(pv_ref, lv_ref, muv_ref)
    wlo = shift_lerp(pw_ref, lw_ref, muw_ref)
    alo = shift_lerp(pa_ref, la_ref, mua_ref)

    wpre = w0_ref[...] + _dot(jnp.tanh(wlo), wup_ref[...])
    ld = -math.exp(-0.5) * _sigmoid(wpre)
    a = _sigmoid(a0_ref[...] + _dot(alo, aup_ref[...]))

    kk = k * kk_ref[...]
    kk = kk / jnp.maximum(jnp.sqrt(head_sum(kk * kk)), L2_EPS)
    k2 = k * (1.0 + (a - 1.0) * ka_ref[...])

    cum = _dot_exact_lhs(tril_t, ld)
    cum_last = cum[L - 1:L, :]
    w_incl = jnp.exp(cum)
    w_excl = jnp.exp(cum - ld)
    w_inv = jnp.exp(-cum)
    w_rest = jnp.exp(cum_last - cum)
    w_all = jnp.stack(lane_tiles(jnp.exp(cum_last)), axis=0)
    kb = kk * a

    a_s = stack(-kk * w_excl)
    r_s = stack(r * w_incl)
    b_s = stack(kb * w_inv)
    k_s = stack(k2 * w_inv)
    v_s = stack(v)
    bw_s = stack(kb * w_rest)
    kw_s = stack(k2 * w_rest)

    lhs = jnp.concatenate([a_s, r_s], axis=1).astype(BF16)
    rhs = jnp.concatenate([b_s, k_s], axis=1).astype(BF16)
    gram = _bmm_nt(lhs, rhs)
    a_ab = jnp.where(strict, gram[:, :n, :n], 0.0)
    a_ak = jnp.where(strict, gram[:, :n, n:], 0.0)
    a_rb = jnp.where(incl, gram[:, n:, :n], 0.0)
    a_rk = jnp.where(incl, gram[:, n:, n:], 0.0)

    pw = a_ab
    inv = eye + pw
    for _ in range(L.bit_length() - 2):
        pw = _bmm(pw, pw)
        inv = inv + _bmm(inv, pw)

    s_old = s_ref[...]
    t1 = _bmm_nt(lhs, s_old)
    u = _bmm(inv, t1[:, :n] + _bmm(a_ak, v_s))
    uv = jnp.concatenate([u, v_s], axis=1).astype(BF16)
    y_s = t1[:, n:] + _bmm(jnp.concatenate([a_rb, a_rk], axis=2), uv)
    s_ref[...] = s_old * w_all + _bmm_tn(uv, jnp.concatenate([bw_s, kw_s], axis=1))
    y3 = y_s[:, :L] + y_s[:, L:]
    y = jnp.concatenate([y3[i] for i in range(pairs)], axis=1)

    yc = y - head_sum(y) * (1.0 / RWKV_HEAD)
    var = head_sum(yc * yc) * (1.0 / RWKV_HEAD)
    yn = yc * lax.rsqrt(var + GN_EPS) * gng_ref[...] + gnb_ref[...]
    bonus = head_sum(r * k2 * rk_ref[...]) * v
    o_ref[0] = ((yn + bonus) * _silu(g_ref[0])).astype(o_ref.dtype)


def _rwkv(p3, mu, w_dec_up, w0, a_up, a0, k_k, k_a, r_k, gn_g, gn_b):
    b, s, _ = p3.shape
    L, pairs = RWKV_CHUNK, RWKV_PAIRS_PER_STEP
    assert 2 * L == LANES
    cw = pairs * LANES
    row = lambda t: t.reshape(1, -1)

    def cols(off):
        return pl.BlockSpec((1, L, cw), lambda bi, gi, ci: (bi, ci, off // cw + gi))

    def lora_cols(off):
        return pl.BlockSpec((1, L, LORA), lambda bi, gi, ci: (bi, ci, off // LORA))

    vec = pl.BlockSpec((1, cw), lambda bi, gi, ci: (0, gi))
    vec_lora = pl.BlockSpec((1, LORA), lambda bi, gi, ci: (0, 0))
    up = pl.BlockSpec((LORA, cw), lambda bi, gi, ci: (0, gi))
    kern = functools.partial(_rwkv_kernel, chunk=L, pairs=pairs)
    return pl.pallas_call(
        kern,
        grid=(b, RWKV_WIDTH // cw, s // L),
        in_specs=[cols(COL_R), cols(COL_K), cols(COL_V), lora_cols(COL_WLO), lora_cols(COL_ALO),
                  cols(COL_GA),
                  vec, vec, vec, vec_lora, vec_lora,
                  up, up, vec, vec, vec, vec, vec, vec, vec],
        out_specs=pl.BlockSpec((1, L, cw), lambda bi, gi, ci: (bi, ci, gi)),
        out_shape=jax.ShapeDtypeStruct((b, s, RWKV_WIDTH), BF16),
        scratch_shapes=[pltpu.VMEM((pairs, LANES, LANES), F32),
                        pltpu.VMEM((1, cw), F32), pltpu.VMEM((1, cw), F32),
                        pltpu.VMEM((1, cw), F32),
                        pltpu.VMEM((1, LORA), F32), pltpu.VMEM((1, LORA), F32)],
        compiler_params=_params(("arbitrary", "arbitrary", "arbitrary")),
        name="rwkv7_mix",
    )(p3, p3, p3, p3, p3, p3,
      row(mu[0:2048]), row(mu[2048:4096]), row(mu[4096:6144]), row(mu[6144:6272]),
      row(mu[6272:6400]),
      w_dec_up.astype(BF16), a_up.astype(BF16), row(w0), row(a0), row(k_k), row(k_a), row(r_k),
      row(gn_g), row(gn_b))


def _sb_kernel(q_ref, k_ref, v_ref, g_ref, o_ref, *, tq, td, tk):
    blk = SB_BLOCK
    s = q_ref.shape[1]
    ri = lax.broadcasted_iota(jnp.int32, (blk, blk), 0)
    ci = lax.broadcasted_iota(jnp.int32, (blk, blk), 1)
    mext = jnp.concatenate([jnp.where(ri > ci, 1.0, 0.0), jnp.ones((blk, blk), F32)],
                           axis=1).astype(BF16)
    causal = (lax.broadcasted_iota(jnp.int32, (td, td), 1)
              < lax.broadcasted_iota(jnp.int32, (td, td), 0))

    def tile(q, k0, width, carry, masked):
        kj = k_ref[0, k0:k0 + width, :]
        vj = v_ref[0, k0:k0 + width, :]
        z = _dot_nt(q, kj)
        lq = jnp.minimum(z, 0.0) - jnp.log(1.0 + jnp.exp(-jnp.abs(z)))
        lk = lq - z
        if masked:
            lk = jnp.where(causal, lk, 0.0)
        lk = lk.astype(BF16)
        later = []
        for i in reversed(range(width // blk)):
            sums = jnp.dot(lk[:, i * blk:(i + 1) * blk], mext, preferred_element_type=F32)
            later.append(sums[:, :blk] + carry)
            carry = carry + sums[:, blk:]
        att = jnp.exp(lq + jnp.concatenate(later[::-1], axis=1))
        if masked:
            att = jnp.where(causal, att, 0.0)
        return jnp.dot(att.astype(BF16), vj, preferred_element_type=F32), carry

    for q0 in range(0, s, tq):
        accs, carries = [], []
        for rb in range(tq // td):
            q = q_ref[0, q0 + rb * td:q0 + (rb + 1) * td, :]
            acc = jnp.zeros((td, SB_HEAD), F32)
            carry = jnp.zeros((td, blk), F32)
            for kb in range(rb, -1, -1):
                part, carry = tile(q, q0 + kb * td, td, carry, kb == rb)
                acc = acc + part
            accs.append(acc)
            carries.append(carry)
        acc = jnp.concatenate(accs, axis=0)
        carry = jnp.concatenate(carries, axis=0)
        q = q_ref[0, q0:q0 + tq, :]
        for k0 in range(q0 - tk, -1, -tk):
            part, carry = tile(q, k0, tk, carry, False)
            acc = acc + part
        gate = _silu(g_ref[0, q0:q0 + tq, :].astype(F32))
        o_ref[0, q0:q0 + tq, :] = (acc * gate).astype(o_ref.dtype)


def _stick_breaking(p3):
    b, s, _ = p3.shape
    tq = min(SB_TQ, s)
    td = min(SB_TD, s)
    tk = min(SB_TK, s)
    assert tq % tk == 0 and tq % td == 0 and s % tq == 0

    def cols(off):
        return pl.BlockSpec((1, s, SB_HEAD), lambda bi, hi: (bi, 0, off // SB_HEAD + hi))

    kern = functools.partial(_sb_kernel, tq=tq, td=td, tk=tk)
    return pl.pallas_call(
        kern,
        grid=(b, SB_HEADS),
        in_specs=[cols(COL_Q), cols(COL_SK), cols(COL_SV), cols(COL_GB)],
        out_specs=pl.BlockSpec((1, s, SB_HEAD), lambda bi, hi: (bi, 0, hi)),
        out_shape=jax.ShapeDtypeStruct((b, s, SB_WIDTH), BF16),
        compiler_params=_params(("arbitrary", "arbitrary")),
        name="stick_breaking",
    )(p3, p3, p3, p3)


def kernel(x, norm_g, final_norm_g, e_w_in, e_shift_mu, e_w_decay_up, e_w0, e_a_up, e_a0,
           e_k_k, e_k_a, e_r_k, e_gn_g, e_gn_b, e_w_out, o_w_in, o_ln_g, o_ln_b, o_w_s,
           o_b_s, o_w_out):
    b, s, d = x.shape
    n = b * s
    x2 = x.reshape(n, d)

    w_in = e_w_in[0]
    w_in_a = jnp.concatenate([w_in[:, :6144], w_in[:, 6400:8448], w_in[:, 6144:6400]], axis=1)
    h = _rmsnorm(x2, norm_g[0], BF16)
    pa = _matmul(h, w_in_a.astype(BF16), 768, F32, _mm_kernel, "even_in_proj_a")
    qscale = functools.partial(_mm_qscale_kernel, n_q=SB_WIDTH // 1024, scale=float(SB_HEAD) ** -0.5)
    pb = _matmul(h, w_in[:, 8448:].astype(BF16), 1024, BF16, qscale, "even_in_proj_b")
    ya = _rwkv(pa.reshape(b, s, EVEN_A_COLS), e_shift_mu[0], e_w_decay_up[0], e_w0[0], e_a_up[0],
               e_a0[0], e_k_k[0], e_k_a[0], e_r_k[0], e_gn_g[0], e_gn_b[0])
    yb = _stick_breaking(pb.reshape(b, s, EVEN_B_COLS))
    x2 = _matmul_res2(ya.reshape(n, RWKV_WIDTH), yb.reshape(n, SB_WIDTH),
                      e_w_out[0].astype(BF16), x2)

    h = _rmsnorm(x2, norm_g[1], BF16)
    w_in = o_w_in[0].astype(BF16)
    uv = _matmul(h, w_in, 1024, BF16, _mm_gelu_kernel, "odd_in_proj_uv", 0, 2 * SGU_WIDTH)
    g = _matmul(h, w_in, 1024, BF16, _mm_silu_kernel, "odd_in_proj_g", 2 * SGU_WIDTH, SGU_WIDTH)
    out = _odd_tail(uv, g, o_ln_g[0], o_ln_b[0], o_w_s[0], o_b_s[0], o_w_out[0].astype(BF16), x2,
                    final_norm_g)
    return out.reshape(b, s, d)
```

```python
import functools
import math

import jax
import jax.numpy as jnp
from jax import lax
from jax.experimental import pallas as pl
from jax.experimental.pallas import tpu as pltpu

F32 = jnp.float32
BF16 = jnp.bfloat16

D_MODEL = 4096
RWKV_WIDTH = 2048
RWKV_HEAD = 64
LORA = 128
SB_WIDTH = 2048
SB_HEAD = 128
SB_HEADS = SB_WIDTH // SB_HEAD
SB_BLOCK = 128
SGU_WIDTH = 4096
SGU_CHUNK = 128
SGU_GROUPS = 16
SGU_GROUP_DIM = SGU_WIDTH // SGU_GROUPS

RMS_EPS = 1e-6
GN_EPS = 64e-5
LN_EPS = 1e-5
L2_EPS = 1e-12

LANES = 128
RWKV_CHUNK = 64
RWKV_PAIRS_PER_STEP = 16
SB_TQ = 512
SB_TD = 256
SB_TK = 512
OUT_TM = 512
OUT_TK = 512
OUT_TN = 1024
VMEM_LIMIT = 56 * 1024 * 1024

COL_R, COL_K, COL_V, COL_WLO, COL_ALO, COL_GA = 0, 2048, 4096, 6144, 6272, 6400
EVEN_A_COLS = 8448
COL_Q, COL_SK, COL_SV, COL_GB = 0, 2048, 4096, 6144
EVEN_B_COLS = 8192


def _params(sem):
    return pltpu.CompilerParams(dimension_semantics=sem, vmem_limit_bytes=VMEM_LIMIT)


def _dot(a, b):
    return jnp.dot(a.astype(BF16), b.astype(BF16), preferred_element_type=F32)


def _dot_nt(a, b):
    return lax.dot_general(a.astype(BF16), b.astype(BF16), (((1,), (1,)), ((), ())),
                           preferred_element_type=F32)


def _split(x):
    hi = x.astype(BF16)
    lo = (x - hi.astype(F32)).astype(BF16)
    return hi, lo


def _dot_exact_rhs(x, m):
    hi, lo = _split(x)
    return jnp.dot(jnp.concatenate([hi, lo], axis=1), jnp.concatenate([m, m], axis=0),
                   preferred_element_type=F32)


def _dot_exact_lhs(m, x):
    hi, lo = _split(x)
    return jnp.dot(jnp.concatenate([m, m], axis=1), jnp.concatenate([hi, lo], axis=0),
                   preferred_element_type=F32)


def _sigmoid(x):
    return 1.0 / (1.0 + jnp.exp(-x))


def _silu(x):
    return x * _sigmoid(x)


def _gelu(x):
    return 0.5 * x * (1.0 + lax.erf(x * (2.0 ** -0.5)))


def _rmsnorm_kernel(x_ref, g_ref, o_ref):
    x = x_ref[...]
    ms = jnp.mean(x * x, axis=-1, keepdims=True)
    o_ref[...] = (x * lax.rsqrt(ms + RMS_EPS) * g_ref[...]).astype(o_ref.dtype)


def _rmsnorm(x2, g, out_dtype):
    n, d = x2.shape
    tm = min(256, n)
    return pl.pallas_call(
        _rmsnorm_kernel,
        grid=(n // tm,),
        in_specs=[pl.BlockSpec((tm, d), lambda i: (i, 0)),
                  pl.BlockSpec((1, d), lambda i: (0, 0))],
        out_specs=pl.BlockSpec((tm, d), lambda i: (i, 0)),
        out_shape=jax.ShapeDtypeStruct((n, d), out_dtype),
        compiler_params=_params(("arbitrary",)),
        name="rmsnorm",
    )(x2, g.reshape(1, d))


def _mm_kernel(a_ref, w_ref, o_ref):
    o_ref[...] = jnp.dot(a_ref[...], w_ref[...], preferred_element_type=F32).astype(o_ref.dtype)


def _mm_qscale_kernel(a_ref, w_ref, o_ref, *, n_q, scale):
    acc = jnp.dot(a_ref[...], w_ref[...], preferred_element_type=F32)
    factor = jnp.where(pl.program_id(1) < n_q, scale, 1.0)
    o_ref[...] = (acc * factor).astype(o_ref.dtype)


def _mm_gelu_kernel(a_ref, w_ref, o_ref):
    acc = jnp.dot(a_ref[...], w_ref[...], preferred_element_type=F32)
    o_ref[...] = _gelu(acc).astype(o_ref.dtype)


def _mm_silu_kernel(a_ref, w_ref, o_ref):
    acc = jnp.dot(a_ref[...], w_ref[...], preferred_element_type=F32)
    o_ref[...] = _silu(acc).astype(o_ref.dtype)


def _matmul(a, w, tn, out_dtype, body, name, col0=0, cols=None):
    n, k = a.shape
    cols = w.shape[1] if cols is None else cols
    assert col0 % LANES == 0 and cols % tn == 0
    tm = min(1024, n)
    return pl.pallas_call(
        body,
        grid=(n // tm, cols // tn),
        in_specs=[pl.BlockSpec((tm, k), lambda i, j: (i, 0)),
                  pl.BlockSpec((pl.Element(k), pl.Element(tn)),
                               lambda i, j: (0, pl.multiple_of(col0 + j * tn, LANES)))],
        out_specs=pl.BlockSpec((tm, tn), lambda i, j: (i, j)),
        out_shape=jax.ShapeDtypeStruct((n, cols), out_dtype),
        compiler_params=_params(("arbitrary", "arbitrary")),
        name=name,
    )(a, w)


def _mm_res2_kernel(a_ref, b_ref, wa_ref, wb_ref, x_ref, o_ref):
    acc = jnp.dot(a_ref[...], wa_ref[...], preferred_element_type=F32)
    acc = acc + jnp.dot(b_ref[...], wb_ref[...], preferred_element_type=F32)
    o_ref[...] = x_ref[...] + acc


def _matmul_res2(a, b, w, x2):
    n, k = a.shape
    cols = w.shape[1]
    tm, tn = min(1024, n), 1024
    return pl.pallas_call(
        _mm_res2_kernel,
        grid=(n // tm, cols // tn),
        in_specs=[pl.BlockSpec((tm, k), lambda i, j: (i, 0)),
                  pl.BlockSpec((tm, k), lambda i, j: (i, 0)),
                  pl.BlockSpec((k, tn), lambda i, j: (0, j)),
                  pl.BlockSpec((k, tn), lambda i, j: (1, j)),
                  pl.BlockSpec((tm, tn), lambda i, j: (i, j))],
        out_specs=pl.BlockSpec((tm, tn), lambda i, j: (i, j)),
        out_shape=jax.ShapeDtypeStruct((n, cols), F32),
        compiler_params=_params(("arbitrary", "arbitrary")),
        name="even_out_proj",
    )(a, b, w, w, x2)


def _slabs(width):
    return [slice(c, c + OUT_TN) for c in range(0, width, OUT_TN)]


def _odd_tail_kernel(u_ref, v_ref, gate_ref, lng_ref, lnb_ref, ws_ref, bs_ref, w_ref, x_ref, fg_ref,
                     o_ref, mu_ref, rstd_ref, *, tk):
    k = pl.program_id(1)
    tm, d = o_ref.shape

    @pl.when(k == 0)
    def _():
        width = v_ref.shape[1]
        mu = sum(jnp.sum(v_ref[:, sl].astype(F32), axis=-1, keepdims=True)
                 for sl in _slabs(width)) * (1.0 / width)
        var = sum(jnp.sum(jnp.square(v_ref[:, sl].astype(F32) - mu), axis=-1, keepdims=True)
                  for sl in _slabs(width)) * (1.0 / width)
        mu_ref[...] = mu
        rstd_ref[...] = lax.rsqrt(var + LN_EPS)
        o_ref[...] = jnp.zeros_like(o_ref)

    vk = v_ref[:, pl.ds(pl.multiple_of(k * tk, tk), tk)].astype(F32)
    vn = ((vk - mu_ref[...]) * rstd_ref[...] * lng_ref[...] + lnb_ref[...]).astype(BF16)
    ri = lax.broadcasted_iota(jnp.int32, (SGU_CHUNK, SGU_CHUNK), 0)
    ci = lax.broadcasted_iota(jnp.int32, (SGU_CHUNK, SGU_CHUNK), 1)
    causal = ci <= ri
    cols = []
    for gg in range(tk // SGU_GROUP_DIM):
        grp = k * (tk // SGU_GROUP_DIM) + gg
        wmix = jnp.where(causal, ws_ref[grp], 0.0).astype(BF16)
        bias = bs_ref[grp]
        sl = slice(gg * SGU_GROUP_DIM, (gg + 1) * SGU_GROUP_DIM)
        rows = [jnp.dot(wmix, vn[c * SGU_CHUNK:(c + 1) * SGU_CHUNK, sl], preferred_element_type=F32) + bias
                for c in range(tm // SGU_CHUNK)]
        cols.append(jnp.concatenate(rows, axis=0))
    mixed = jnp.concatenate(cols, axis=1)
    y = (u_ref[...].astype(F32) * gate_ref[...].astype(F32) * mixed).astype(BF16)
    for sl in _slabs(d):
        o_ref[:, sl] += jnp.dot(y, w_ref[:, sl], preferred_element_type=F32)
    res_cols = pl.ds(pl.multiple_of(k * tk, tk), tk)
    o_ref[:, res_cols] += x_ref[...]

    @pl.when(k == pl.num_programs(1) - 1)
    def _():
        ss = sum(jnp.sum(jnp.square(o_ref[:, sl]), axis=-1, keepdims=True) for sl in _slabs(d))
        r = lax.rsqrt(ss * (1.0 / d) + RMS_EPS)
        for sl in _slabs(d):
            o_ref[:, sl] = o_ref[:, sl] * r * fg_ref[:, sl]


def _odd_tail(uv, gate, ln_g, ln_b, w_s, b_s, w, x2, fg):
    n, d = x2.shape
    tm, tk = min(OUT_TM, n), OUT_TK
    assert tm % SGU_CHUNK == 0 and tk % SGU_GROUP_DIM == 0 and d == SGU_WIDTH
    row = pl.BlockSpec((tm, d), lambda i, k: (i, 0))
    chunk = pl.BlockSpec((tm, tk), lambda i, k: (i, k))
    kvec = pl.BlockSpec((1, tk), lambda i, k: (0, k))
    return pl.pallas_call(
        functools.partial(_odd_tail_kernel, tk=tk),
        grid=(n // tm, SGU_WIDTH // tk),
        in_specs=[chunk,
                  pl.BlockSpec((tm, SGU_WIDTH), lambda i, k: (i, 1)),
                  chunk,
                  kvec, kvec,
                  pl.BlockSpec((SGU_GROUPS, SGU_CHUNK, SGU_CHUNK), lambda i, k: (0, 0, 0)),
                  pl.BlockSpec((SGU_GROUPS, SGU_CHUNK, 1), lambda i, k: (0, 0, 0)),
                  pl.BlockSpec((tk, d), lambda i, k: (k, 0)),
                  chunk,
                  pl.BlockSpec((1, d), lambda i, k: (0, 0))],
        out_specs=row,
        out_shape=jax.ShapeDtypeStruct((n, d), F32),
        scratch_shapes=[pltpu.VMEM((tm, 1), F32), pltpu.VMEM((tm, 1), F32)],
        compiler_params=_params(("arbitrary", "arbitrary")),
        name="odd_tail",
    )(uv, uv, gate, ln_g.reshape(1, -1), ln_b.reshape(1, -1), w_s, b_s[:, :, None], w, x2,
      fg.reshape(1, d))


def _bmm(a, b):
    return jnp.einsum("gij,gjk->gik", a.astype(BF16), b.astype(BF16), preferred_element_type=F32)


def _bmm_nt(a, b):
    return jnp.einsum("gik,gjk->gij", a.astype(BF16), b.astype(BF16), preferred_element_type=F32)


def _bmm_tn(a, b):
    return jnp.einsum("gki,gkj->gij", a.astype(BF16), b.astype(BF16), preferred_element_type=F32)


def _rwkv_kernel(pr_ref, pk_ref, pv_ref, pw_ref, pa_ref, g_ref,
                 mur_ref, muk_ref, muv_ref, muw_ref, mua_ref,
                 wup_ref, aup_ref, w0_ref, a0_ref, kk_ref, ka_ref, rk_ref,
                 gng_ref, gnb_ref,
                 o_ref,
                 s_ref, lr_ref, lk_ref, lv_ref, lw_ref, la_ref, *, chunk, pairs):
    L = chunk
    n = 2 * L
    c = pl.program_id(2)

    @pl.when(c == 0)
    def _():
        s_ref[...] = jnp.zeros_like(s_ref)
        lr_ref[...] = jnp.zeros_like(lr_ref)
        lk_ref[...] = jnp.zeros_like(lk_ref)
        lv_ref[...] = jnp.zeros_like(lv_ref)
        lw_ref[...] = jnp.zeros_like(lw_ref)
        la_ref[...] = jnp.zeros_like(la_ref)

    row = lax.broadcasted_iota(jnp.int32, (L, 1), 0)

    def shift_lerp(x_ref, last_ref, mu_ref):
        x = x_ref[0]
        prev = jnp.where(row == 0, last_ref[...], pltpu.roll(x, 1, 0))
        last_ref[...] = x[L - 1:L, :]
        return x + mu_ref[...] * (prev - x)

    ti = lax.broadcasted_iota(jnp.int32, (L, L), 0)
    tj = lax.broadcasted_iota(jnp.int32, (L, L), 1)
    tril_t = jnp.where(tj <= ti, 1.0, 0.0).astype(BF16)
    ei = lax.broadcasted_iota(jnp.int32, (LANES, LANES), 0)
    ej = lax.broadcasted_iota(jnp.int32, (LANES, LANES), 1)
    seg = jnp.where((ei // RWKV_HEAD) == (ej // RWKV_HEAD), 1.0, 0.0).astype(BF16)
    hmask = jnp.where((ei // L) == (ej // RWKV_HEAD), 1.0, 0.0)
    strict = ej < ei
    incl = ej <= ei
    eye = jnp.where(ei == ej, 1.0, 0.0)

    def lane_tiles(x):
        return [x[:, i * LANES:(i + 1) * LANES] for i in range(pairs)]

    def head_sum(x):
        rows = jnp.concatenate(lane_tiles(x), axis=0)
        sums = _dot_exact_rhs(rows, seg)
        return jnp.concatenate([sums[i * L:(i + 1) * L] for i in range(pairs)], axis=1)

    def stack(x):
        xs = jnp.stack(lane_tiles(x), axis=0)
        return jnp.concatenate([xs, xs], axis=1) * hmask

    r = shift_lerp(pr_ref, lr_ref, mur_ref)
    k = shift_lerp(pk_ref, lk_ref, muk_ref)
    v = shift_lerp(pv_ref, lv_ref, muv_ref)
    wlo = shift_lerp(pw_ref, lw_ref, muw_ref)
    alo = shift_lerp(pa_ref, la_ref, mua_ref)

    wpre = w0_ref[...] + _dot(jnp.tanh(wlo), wup_ref[...])
    ld = -math.exp(-0.5) * _sigmoid(wpre)
    a = _sigmoid(a0_ref[...] + _dot(alo, aup_ref[...]))

    kk = k * kk_ref[...]
    kk = kk / jnp.maximum(jnp.sqrt(head_sum(kk * kk)), L2_EPS)
    k2 = k * (1.0 + (a - 1.0) * ka_ref[...])

    cum = _dot_exact_lhs(tril_t, ld)
    cum_last = cum[L - 1:L, :]
    w_incl = jnp.exp(cum)
    w_excl = jnp.exp(cum - ld)
    w_inv = jnp.exp(-cum)
    w_rest = jnp.exp(cum_last - cum)
    w_all = jnp.stack(lane_tiles(jnp.exp(cum_last)), axis=0)
    kb = kk * a

    a_s = stack(-kk * w_excl)
    r_s = stack(r * w_incl)
    b_s = stack(kb * w_inv)
    k_s = stack(k2 * w_inv)
    v_s = stack(v)
    bw_s = stack(kb * w_rest)
    kw_s = stack(k2 * w_rest)

    lhs = jnp.concatenate([a_s, r_s], axis=1).astype(BF16)
    rhs = jnp.concatenate([b_s, k_s], axis=1).astype(BF16)
    gram = _bmm_nt(lhs, rhs)
    a_ab = jnp.where(strict, gram[:, :n, :n], 0.0)
    a_ak = jnp.where(strict, gram[:, :n, n:], 0.0)
    a_rb = jnp.where(incl, gram[:, n:, :n], 0.0)
    a_rk = jnp.where(incl, gram[:, n:, n:], 0.0)

    pw = a_ab
    inv = eye + pw
    for _ in range(L.bit_length() - 2):
        pw = _bmm(pw, pw)
        inv = inv + _bmm(inv, pw)

    s_old = s_ref[...]
    t1 = _bmm_nt(lhs, s_old)
    u = _bmm(inv, t1[:, :n] + _bmm(a_ak, v_s))
    uv = jnp.concatenate([u, v_s], axis=1).astype(BF16)
    y_s = t1[:, n:] + _bmm(jnp.concatenate([a_rb, a_rk], axis=2), uv)
    s_ref[...] = s_old * w_all + _bmm_tn(uv, jnp.concatenate([bw_s, kw_s], axis=1))
    y3 = y_s[:, :L] + y_s[:, L:]
    y = jnp.concatenate([y3[i] for i in range(pairs)], axis=1)

    yc = y - head_sum(y) * (1.0 / RWKV_HEAD)
    var = head_sum(yc * yc) * (1.0 / RWKV_HEAD)
    yn = yc * lax.rsqrt(var + GN_EPS) * gng_ref[...] + gnb_ref[...]
    bonus = head_sum(r * k2 * rk_ref[...]) * v
    o_ref[0] = ((yn + bonus) * _silu(g_ref[0])).astype(o_ref.dtype)


def _rwkv(p3, mu, w_dec_up, w0, a_up, a0, k_k, k_a, r_k, gn_g, gn_b):
    b, s, _ = p3.shape
    L, pairs = RWKV_CHUNK, RWKV_PAIRS_PER_STEP
    assert 2 * L == LANES
    cw = pairs * LANES
    row = lambda t: t.reshape(1, -1)

    def cols(off):
        return pl.BlockSpec((pl.Element(1), pl.Element(L), pl.Element(cw)),
                            lambda bi, gi, ci: (bi, pl.multiple_of(ci * L, L),
                                                pl.multiple_of(off + gi * cw, LANES)))

    def lora_cols(off):
        return pl.BlockSpec((1, L, LORA), lambda bi, gi, ci: (bi, ci, off // LORA))

    vec = pl.BlockSpec((1, cw), lambda bi, gi, ci: (0, gi))
    vec_lora = pl.BlockSpec((1, LORA), lambda bi, gi, ci: (0, 0))
    up = pl.BlockSpec((LORA, cw), lambda bi, gi, ci: (0, gi))
    kern = functools.partial(_rwkv_kernel, chunk=L, pairs=pairs)
    return pl.pallas_call(
        kern,
        grid=(b, RWKV_WIDTH // cw, s // L),
        in_specs=[cols(COL_R), cols(COL_K), cols(COL_V), lora_cols(COL_WLO), lora_cols(COL_ALO),
                  cols(COL_GA),
                  vec, vec, vec, vec_lora, vec_lora,
                  up, up, vec, vec, vec, vec, vec, vec, vec],
        out_specs=pl.BlockSpec((1, L, cw), lambda bi, gi, ci: (bi, ci, gi)),
        out_shape=jax.ShapeDtypeStruct((b, s, RWKV_WIDTH), BF16),
        scratch_shapes=[pltpu.VMEM((pairs, LANES, LANES), F32),
                        pltpu.VMEM((1, cw), F32), pltpu.VMEM((1, cw), F32),
                        pltpu.VMEM((1, cw), F32),
                        pltpu.VMEM((1, LORA), F32), pltpu.VMEM((1, LORA), F32)],
        compiler_params=_params(("arbitrary", "arbitrary", "arbitrary")),
        name="rwkv7_mix",
    )(p3, p3, p3, p3, p3, p3,
      row(mu[0:2048]), row(mu[2048:4096]), row(mu[4096:6144]), row(mu[6144:6272]),
      row(mu[6272:6400]),
      w_dec_up.astype(BF16), a_up.astype(BF16), row(w0), row(a0), row(k_k), row(k_a), row(r_k),
      row(gn_g), row(gn_b))


def _sb_kernel(q_ref, k_ref, v_ref, g_ref, o_ref, *, tq, td, tk):
    blk = SB_BLOCK
    s = q_ref.shape[1]
    ri = lax.broadcasted_iota(jnp.int32, (blk, blk), 0)
    ci = lax.broadcasted_iota(jnp.int32, (blk, blk), 1)
    mext = jnp.concatenate([jnp.where(ri > ci, 1.0, 0.0), jnp.ones((blk, blk), F32)],
                           axis=1).astype(BF16)
    causal = (lax.broadcasted_iota(jnp.int32, (td, td), 1)
              < lax.broadcasted_iota(jnp.int32, (td, td), 0))

    def tile(q, k0, width, carry, masked):
        kj = k_ref[0, k0:k0 + width, :]
        vj = v_ref[0, k0:k0 + width, :]
        z = _dot_nt(q, kj).astype(BF16)
        lq = jnp.minimum(z, 0.0) - jnp.log(1.0 + jnp.exp(-jnp.abs(z)))
        lk = lq - z
        if masked:
            lk = jnp.where(causal, lk, jnp.zeros_like(lk))
        later = []
        for i in reversed(range(width // blk)):
            sums = jnp.dot(lk[:, i * blk:(i + 1) * blk], mext, preferred_element_type=F32)
            later.append((sums[:, :blk] + carry).astype(BF16))
            carry = carry + sums[:, blk:]
        att = jnp.exp(lq + jnp.concatenate(later[::-1], axis=1))
        if masked:
            att = jnp.where(causal, att, jnp.zeros_like(att))
        return jnp.dot(att, vj, preferred_element_type=F32), carry

    for q0 in range(0, s, tq):
        accs, carries = [], []
        for rb in range(tq // td):
            q = q_ref[0, q0 + rb * td:q0 + (rb + 1) * td, :]
            acc = jnp.zeros((td, SB_HEAD), F32)
            carry = jnp.zeros((td, blk), F32)
            for kb in range(rb, -1, -1):
                part, carry = tile(q, q0 + kb * td, td, carry, kb == rb)
                acc = acc + part
            accs.append(acc)
            carries.append(carry)
        acc = jnp.concatenate(accs, axis=0)
        carry = jnp.concatenate(carries, axis=0)
        q = q_ref[0, q0:q0 + tq, :]
        for k0 in range(q0 - tk, -1, -tk):
            part, carry = tile(q, k0, tk, carry, False)
            acc = acc + part
        gate = _silu(g_ref[0, q0:q0 + tq, :].astype(F32))
        o_ref[0, q0:q0 + tq, :] = (acc * gate).astype(o_ref.dtype)


def _stick_breaking(p3):
    b, s, _ = p3.shape
    tq = min(SB_TQ, s)
    td = min(SB_TD, s)
    tk = min(SB_TK, s)
    assert tq % tk == 0 and tq % td == 0 and s % tq == 0

    def cols(off):
        return pl.BlockSpec((1, s, SB_HEAD), lambda bi, hi: (bi, 0, off // SB_HEAD + hi))

    kern = functools.partial(_sb_kernel, tq=tq, td=td, tk=tk)
    return pl.pallas_call(
        kern,
        grid=(b, SB_HEADS),
        in_specs=[cols(COL_Q), cols(COL_SK), cols(COL_SV), cols(COL_GB)],
        out_specs=pl.BlockSpec((1, s, SB_HEAD), lambda bi, hi: (bi, 0, hi)),
        out_shape=jax.ShapeDtypeStruct((b, s, SB_WIDTH), BF16),
        compiler_params=_params(("arbitrary", "arbitrary")),
        name="stick_breaking",
    )(p3, p3, p3, p3)


def kernel(x, norm_g, final_norm_g, e_w_in, e_shift_mu, e_w_decay_up, e_w0, e_a_up, e_a0,
           e_k_k, e_k_a, e_r_k, e_gn_g, e_gn_b, e_w_out, o_w_in, o_ln_g, o_ln_b, o_w_s,
           o_b_s, o_w_out):
    b, s, d = x.shape
    n = b * s
    x2 = x.reshape(n, d)
    assert e_w_in.shape[0] == 1 and o_w_in.shape[0] == 1

    w_in = e_w_in.reshape(d, -1).astype(BF16)
    h = _rmsnorm(x2, norm_g[0], BF16)
    pa = _matmul(h, w_in, 768, F32, _mm_kernel, "even_in_proj_a", 0, EVEN_A_COLS)
    qscale = functools.partial(_mm_qscale_kernel, n_q=SB_WIDTH // 1024, scale=float(SB_HEAD) ** -0.5)
    pb = _matmul(h, w_in, 1024, BF16, qscale, "even_in_proj_b", EVEN_A_COLS, EVEN_B_COLS)
    ya = _rwkv(pa.reshape(b, s, EVEN_A_COLS), e_shift_mu[0], e_w_decay_up[0], e_w0[0], e_a_up[0],
               e_a0[0], e_k_k[0], e_k_a[0], e_r_k[0], e_gn_g[0], e_gn_b[0])
    yb = _stick_breaking(pb.reshape(b, s, EVEN_B_COLS))
    x2 = _matmul_res2(ya.reshape(n, RWKV_WIDTH), yb.reshape(n, SB_WIDTH),
                      e_w_out.reshape(d, d).astype(BF16), x2)

    h = _rmsnorm(x2, norm_g[1], BF16)
    w_in = o_w_in.reshape(d, -1).astype(BF16)
    uv = _matmul(h, w_in, 1024, BF16, _mm_gelu_kernel, "odd_in_proj_uv", 0, 2 * SGU_WIDTH)
    g = _matmul(h, w_in, 1024, BF16, _mm_silu_kernel, "odd_in_proj_g", 2 * SGU_WIDTH, SGU_WIDTH)
    out = _odd_tail(uv, g, o_ln_g[0], o_ln_b[0], o_w_s[0], o_b_s[0],
                    o_w_out.reshape(d, d).astype(BF16), x2, final_norm_g)
    return out.reshape(b, s, d)
```

```python
import functools
import math

import jax
import jax.numpy as jnp
from jax import lax
from jax.experimental import pallas as pl
from jax.experimental.pallas import tpu as pltpu

F32 = jnp.float32
BF16 = jnp.bfloat16

D_MODEL = 4096
RWKV_WIDTH = 2048
RWKV_HEAD = 64
LORA = 128
SB_WIDTH = 2048
SB_HEAD = 128
SB_HEADS = SB_WIDTH // SB_HEAD
SB_BLOCK = 128
SGU_WIDTH = 4096
SGU_CHUNK = 128
SGU_GROUPS = 16
SGU_GROUP_DIM = SGU_WIDTH // SGU_GROUPS

RMS_EPS = 1e-6
GN_EPS = 64e-5
LN_EPS = 1e-5
L2_EPS = 1e-12

LANES = 128
RWKV_CHUNK = 64
RWKV_PAIRS_PER_STEP = 16
SB_TQ = 512
SB_TD = 256
SB_TK = 512
CAST_ROWS = 32
OUT_TM = 512
OUT_TK = 512
OUT_TN = 1024
VMEM_LIMIT = 56 * 1024 * 1024

COL_R, COL_K, COL_V, COL_WLO, COL_ALO, COL_GA = 0, 2048, 4096, 6144, 6272, 6400
EVEN_A_COLS = 8448
COL_Q, COL_SK, COL_SV, COL_GB = 0, 2048, 4096, 6144
EVEN_B_COLS = 8192


def _params(sem):
    return pltpu.CompilerParams(dimension_semantics=sem, vmem_limit_bytes=VMEM_LIMIT)


def _dot(a, b):
    return jnp.dot(a.astype(BF16), b.astype(BF16), preferred_element_type=F32)


def _dot_nt(a, b):
    return lax.dot_general(a.astype(BF16), b.astype(BF16), (((1,), (1,)), ((), ())),
                           preferred_element_type=F32)


def _split(x):
    hi = x.astype(BF16)
    lo = (x - hi.astype(F32)).astype(BF16)
    return hi, lo


def _dot_exact_rhs(x, m):
    hi, lo = _split(x)
    return jnp.dot(jnp.concatenate([hi, lo], axis=1), jnp.concatenate([m, m], axis=0),
                   preferred_element_type=F32)


def _dot_exact_lhs(m, x):
    hi, lo = _split(x)
    return jnp.dot(jnp.concatenate([m, m], axis=1), jnp.concatenate([hi, lo], axis=0),
                   preferred_element_type=F32)


def _sigmoid(x):
    return 1.0 / (1.0 + jnp.exp(-x))


def _silu(x):
    return x * _sigmoid(x)


def _gelu(x):
    return 0.5 * x * (1.0 + lax.erf(x * (2.0 ** -0.5)))


def _rmsnorm_kernel(x_ref, g_ref, o_ref):
    x = x_ref[...]
    ms = jnp.mean(x * x, axis=-1, keepdims=True)
    o_ref[...] = (x * lax.rsqrt(ms + RMS_EPS) * g_ref[...]).astype(o_ref.dtype)


def _rmsnorm(x2, g, out_dtype):
    n, d = x2.shape
    tm = min(256, n)
    return pl.pallas_call(
        _rmsnorm_kernel,
        grid=(n // tm,),
        in_specs=[pl.BlockSpec((tm, d), lambda i: (i, 0)),
                  pl.BlockSpec((1, d), lambda i: (0, 0))],
        out_specs=pl.BlockSpec((tm, d), lambda i: (i, 0)),
        out_shape=jax.ShapeDtypeStruct((n, d), out_dtype),
        compiler_params=_params(("arbitrary",)),
        name="rmsnorm",
    )(x2, g.reshape(1, d))


def _ep_plain(acc, j):
    return acc


def _ep_gelu(acc, j):
    return _gelu(acc)


def _ep_silu(acc, j):
    return _silu(acc)


def _mm_kernel(*refs, epilogue, n_casts):
    a_ref, w_ref = refs[:2]
    srcs = refs[2:2 + n_casts]
    o_ref = refs[2 + n_casts]
    dsts = refs[3 + n_casts:]
    acc = jnp.dot(a_ref[...], w_ref[...], preferred_element_type=F32)
    o_ref[...] = epilogue(acc, pl.program_id(1)).astype(o_ref.dtype)
    for src, dst in zip(srcs, dsts):
        dst[...] = src[...].astype(dst.dtype)


def _matmul(a, w, tn, out_dtype, epilogue, name, col0=0, cols=None, casts=()):
    n, k = a.shape
    cols = w.shape[1] if cols is None else cols
    assert col0 % LANES == 0 and cols % tn == 0
    tm = min(1024, n)
    nj = cols // tn
    steps = (n // tm) * nj
    in_specs = [pl.BlockSpec((tm, k), lambda i, j: (i, 0)),
                pl.BlockSpec((pl.Element(k), pl.Element(tn)),
                             lambda i, j: (0, pl.multiple_of(col0 + j * tn, LANES)))]
    out_specs = [pl.BlockSpec((tm, tn), lambda i, j: (i, j))]
    out_shape = [jax.ShapeDtypeStruct((n, cols), out_dtype)]
    for src in casts:
        rows, width = src.shape
        nslab = 1 << (min(steps, rows // CAST_ROWS).bit_length() - 1)
        assert rows % nslab == 0
        slab = pl.BlockSpec((rows // nslab, width),
                            lambda i, j, nslab=nslab: (jnp.minimum(i * nj + j, nslab - 1), 0))
        in_specs.append(slab)
        out_specs.append(slab)
        out_shape.append(jax.ShapeDtypeStruct((rows, width), BF16))
    outs = pl.pallas_call(
        functools.partial(_mm_kernel, epilogue=epilogue, n_casts=len(casts)),
        grid=(n // tm, nj),
        in_specs=in_specs,
        out_specs=out_specs,
        out_shape=out_shape,
        compiler_params=_params(("arbitrary", "arbitrary")),
        name=name,
    )(a, w, *casts)
    return outs if casts else outs[0]


def _mm_res2_kernel(a_ref, b_ref, wa_ref, wb_ref, x_ref, o_ref):
    acc = jnp.dot(a_ref[...], wa_ref[...], preferred_element_type=F32)
    acc = acc + jnp.dot(b_ref[...], wb_ref[...], preferred_element_type=F32)
    o_ref[...] = x_ref[...] + acc


def _matmul_res2(a, b, w, x2):
    n, k = a.shape
    cols = w.shape[1]
    tm, tn = min(1024, n), 1024
    return pl.pallas_call(
        _mm_res2_kernel,
        grid=(n // tm, cols // tn),
        in_specs=[pl.BlockSpec((tm, k), lambda i, j: (i, 0)),
                  pl.BlockSpec((tm, k), lambda i, j: (i, 0)),
                  pl.BlockSpec((k, tn), lambda i, j: (0, j)),
                  pl.BlockSpec((k, tn), lambda i, j: (1, j)),
                  pl.BlockSpec((tm, tn), lambda i, j: (i, j))],
        out_specs=pl.BlockSpec((tm, tn), lambda i, j: (i, j)),
        out_shape=jax.ShapeDtypeStruct((n, cols), F32),
        compiler_params=_params(("arbitrary", "arbitrary")),
        name="even_out_proj",
    )(a, b, w, w, x2)


def _slabs(width):
    return [slice(c, c + OUT_TN) for c in range(0, width, OUT_TN)]


def _odd_tail_kernel(u_ref, v_ref, gate_ref, lng_ref, lnb_ref, ws_ref, bs_ref, w_ref, x_ref, fg_ref,
                     o_ref, mu_ref, rstd_ref, *, tk):
    k = pl.program_id(1)
    tm, d = o_ref.shape

    @pl.when(k == 0)
    def _():
        width = v_ref.shape[1]
        mu = sum(jnp.sum(v_ref[:, sl].astype(F32), axis=-1, keepdims=True)
                 for sl in _slabs(width)) * (1.0 / width)
        var = sum(jnp.sum(jnp.square(v_ref[:, sl].astype(F32) - mu), axis=-1, keepdims=True)
                  for sl in _slabs(width)) * (1.0 / width)
        mu_ref[...] = mu
        rstd_ref[...] = lax.rsqrt(var + LN_EPS)
        o_ref[...] = jnp.zeros_like(o_ref)

    vk = v_ref[:, pl.ds(pl.multiple_of(k * tk, tk), tk)].astype(F32)
    vn = ((vk - mu_ref[...]) * rstd_ref[...] * lng_ref[...] + lnb_ref[...]).astype(BF16)
    ri = lax.broadcasted_iota(jnp.int32, (SGU_CHUNK, SGU_CHUNK), 0)
    ci = lax.broadcasted_iota(jnp.int32, (SGU_CHUNK, SGU_CHUNK), 1)
    causal = ci <= ri
    cols = []
    for gg in range(tk // SGU_GROUP_DIM):
        grp = k * (tk // SGU_GROUP_DIM) + gg
        wmix = jnp.where(causal, ws_ref[grp], 0.0).astype(BF16)
        bias = bs_ref[grp]
        sl = slice(gg * SGU_GROUP_DIM, (gg + 1) * SGU_GROUP_DIM)
        rows = [jnp.dot(wmix, vn[c * SGU_CHUNK:(c + 1) * SGU_CHUNK, sl], preferred_element_type=F32) + bias
                for c in range(tm // SGU_CHUNK)]
        cols.append(jnp.concatenate(rows, axis=0))
    mixed = jnp.concatenate(cols, axis=1)
    y = (u_ref[...].astype(F32) * gate_ref[...].astype(F32) * mixed).astype(BF16)
    for sl in _slabs(d):
        o_ref[:, sl] += jnp.dot(y, w_ref[:, sl], preferred_element_type=F32)
    res_cols = pl.ds(pl.multiple_of(k * tk, tk), tk)
    o_ref[:, res_cols] += x_ref[...]

    @pl.when(k == pl.num_programs(1) - 1)
    def _():
        ss = sum(jnp.sum(jnp.square(o_ref[:, sl]), axis=-1, keepdims=True) for sl in _slabs(d))
        r = lax.rsqrt(ss * (1.0 / d) + RMS_EPS)
        for sl in _slabs(d):
            o_ref[:, sl] = o_ref[:, sl] * r * fg_ref[:, sl]


def _odd_tail(uv, gate, ln_g, ln_b, w_s, b_s, w, x2, fg):
    n, d = x2.shape
    tm, tk = min(OUT_TM, n), OUT_TK
    assert tm % SGU_CHUNK == 0 and tk % SGU_GROUP_DIM == 0 and d == SGU_WIDTH
    row = pl.BlockSpec((tm, d), lambda i, k: (i, 0))
    chunk = pl.BlockSpec((tm, tk), lambda i, k: (i, k))
    kvec = pl.BlockSpec((1, tk), lambda i, k: (0, k))
    return pl.pallas_call(
        functools.partial(_odd_tail_kernel, tk=tk),
        grid=(n // tm, SGU_WIDTH // tk),
        in_specs=[chunk,
                  pl.BlockSpec((tm, SGU_WIDTH), lambda i, k: (i, 1)),
                  chunk,
                  kvec, kvec,
                  pl.BlockSpec((SGU_GROUPS, SGU_CHUNK, SGU_CHUNK), lambda i, k: (0, 0, 0)),
                  pl.BlockSpec((SGU_GROUPS, SGU_CHUNK, 1), lambda i, k: (0, 0, 0)),
                  pl.BlockSpec((tk, d), lambda i, k: (k, 0)),
                  chunk,
                  pl.BlockSpec((1, d), lambda i, k: (0, 0))],
        out_specs=row,
        out_shape=jax.ShapeDtypeStruct((n, d), F32),
        scratch_shapes=[pltpu.VMEM((tm, 1), F32), pltpu.VMEM((tm, 1), F32)],
        compiler_params=_params(("arbitrary", "arbitrary")),
        name="odd_tail",
    )(uv, uv, gate, ln_g.reshape(1, -1), ln_b.reshape(1, -1), w_s, b_s[:, :, None], w, x2,
      fg.reshape(1, d))


def _bmm(a, b):
    return jnp.einsum("gij,gjk->gik", a.astype(BF16), b.astype(BF16), preferred_element_type=F32)


def _bmm_nt(a, b):
    return jnp.einsum("gik,gjk->gij", a.astype(BF16), b.astype(BF16), preferred_element_type=F32)


def _bmm_tn(a, b):
    return jnp.einsum("gki,gkj->gij", a.astype(BF16), b.astype(BF16), preferred_element_type=F32)


def _rwkv_kernel(pr_ref, pk_ref, pv_ref, pw_ref, pa_ref, g_ref,
                 mur_ref, muk_ref, muv_ref, muw_ref, mua_ref,
                 wup_ref, aup_ref, w0_ref, a0_ref, kk_ref, ka_ref, rk_ref,
                 gng_ref, gnb_ref,
                 o_ref,
                 s_ref, lr_ref, lk_ref, lv_ref, lw_ref, la_ref, *, chunk, pairs):
    L = chunk
    n = 2 * L
    c = pl.program_id(2)

    @pl.when(c == 0)
    def _():
        s_ref[...] = jnp.zeros_like(s_ref)
        lr_ref[...] = jnp.zeros_like(lr_ref)
        lk_ref[...] = jnp.zeros_like(lk_ref)
        lv_ref[...] = jnp.zeros_like(lv_ref)
        lw_ref[...] = jnp.zeros_like(lw_ref)
        la_ref[...] = jnp.zeros_like(la_ref)

    row = lax.broadcasted_iota(jnp.int32, (L, 1), 0)

    def shift_lerp(x_ref, last_ref, mu_ref):
        x = x_ref[0]
        prev = jnp.where(row == 0, last_ref[...], pltpu.roll(x, 1, 0))
        last_ref[...] = x[L - 1:L, :]
        return x + mu_ref[...] * (prev - x)

    ti = lax.broadcasted_iota(jnp.int32, (L, L), 0)
    tj = lax.broadcasted_iota(jnp.int32, (L, L), 1)
    tril_t = jnp.where(tj <= ti, 1.0, 0.0).astype(BF16)
    ei = lax.broadcasted_iota(jnp.int32, (LANES, LANES), 0)
    ej = lax.broadcasted_iota(jnp.int32, (LANES, LANES), 1)
    seg = jnp.where((ei // RWKV_HEAD) == (ej // RWKV_HEAD), 1.0, 0.0).astype(BF16)
    hmask = jnp.where((ei // L) == (ej // RWKV_HEAD), 1.0, 0.0)
    strict = ej < ei
    incl = ej <= ei
    eye = jnp.where(ei == ej, 1.0, 0.0)

    def lane_tiles(x):
        return [x[:, i * LANES:(i + 1) * LANES] for i in range(pairs)]

    def head_sum(x):
        rows = jnp.concatenate(lane_tiles(x), axis=0)
        sums = _dot_exact_rhs(rows, seg)
        return jnp.concatenate([sums[i * L:(i + 1) * L] for i in range(pairs)], axis=1)

    def stack(x):
        xs = jnp.stack(lane_tiles(x), axis=0)
        return jnp.concatenate([xs, xs], axis=1) * hmask

    r = shift_lerp(pr_ref, lr_ref, mur_ref)
    k = shift_lerp(pk_ref, lk_ref, muk_ref)
    v = shift_lerp(pv_ref, lv_ref, muv_ref)
    wlo = shift_lerp(pw_ref, lw_ref, muw_ref)
    alo = shift_lerp(pa_ref, la_ref, mua_ref)

    wpre = w0_ref[...] + _dot(jnp.tanh(wlo), wup_ref[...])
    ld = -math.exp(-0.5) * _sigmoid(wpre)
    a = _sigmoid(a0_ref[...] + _dot(alo, aup_ref[...]))

    kk = k * kk_ref[...]
    kk = kk / jnp.maximum(jnp.sqrt(head_sum(kk * kk)), L2_EPS)
    k2 = k * (1.0 + (a - 1.0) * ka_ref[...])

    cum = _dot_exact_lhs(tril_t, ld)
    cum_last = cum[L - 1:L, :]
    w_incl = jnp.exp(cum)
    w_excl = jnp.exp(cum - ld)
    w_inv = jnp.exp(-cum)
    w_rest = jnp.exp(cum_last - cum)
    w_all = jnp.stack(lane_tiles(jnp.exp(cum_last)), axis=0)
    kb = kk * a

    a_s = stack(-kk * w_excl)
    r_s = stack(r * w_incl)
    b_s = stack(kb * w_inv)
    k_s = stack(k2 * w_inv)
    v_s = stack(v)
    bw_s = stack(kb * w_rest)
    kw_s = stack(k2 * w_rest)

    lhs = jnp.concatenate([a_s, r_s], axis=1).astype(BF16)
    rhs = jnp.concatenate([b_s, k_s], axis=1).astype(BF16)
    gram = _bmm_nt(lhs, rhs)
    a_ab = jnp.where(strict, gram[:, :n, :n], 0.0)
    a_ak = jnp.where(strict, gram[:, :n, n:], 0.0)
    a_rb = jnp.where(incl, gram[:, n:, :n], 0.0)
    a_rk = jnp.where(incl, gram[:, n:, n:], 0.0)

    pw = a_ab
    inv = eye + pw
    for _ in range(L.bit_length() - 2):
        pw = _bmm(pw, pw)
        inv = inv + _bmm(inv, pw)

    s_old = s_ref[...]
    t1 = _bmm_nt(lhs, s_old)
    u = _bmm(inv, t1[:, :n] + _bmm(a_ak, v_s))
    uv = jnp.concatenate([u, v_s], axis=1).astype(BF16)
    y_s = t1[:, n:] + _bmm(jnp.concatenate([a_rb, a_rk], axis=2), uv)
    s_ref[...] = s_old * w_all + _bmm_tn(uv, jnp.concatenate([bw_s, kw_s], axis=1))
    y3 = y_s[:, :L] + y_s[:, L:]
    y = jnp.concatenate([y3[i] for i in range(pairs)], axis=1)

    yc = y - head_sum(y) * (1.0 / RWKV_HEAD)
    var = head_sum(yc * yc) * (1.0 / RWKV_HEAD)
    yn = yc * lax.rsqrt(var + GN_EPS) * gng_ref[...] + gnb_ref[...]
    bonus = head_sum(r * k2 * rk_ref[...]) * v
    o_ref[0] = ((yn + bonus) * _silu(g_ref[0])).astype(o_ref.dtype)


def _rwkv(p3, mu, w_dec_up, w0, a_up, a0, k_k, k_a, r_k, gn_g, gn_b):
    b, s, _ = p3.shape
    L, pairs = RWKV_CHUNK, RWKV_PAIRS_PER_STEP
    assert 2 * L == LANES
    cw = pairs * LANES
    row = lambda t: t.reshape(1, -1)

    def cols(off):
        return pl.BlockSpec((pl.Element(1), pl.Element(L), pl.Element(cw)),
                            lambda bi, gi, ci: (bi, pl.multiple_of(ci * L, L),
                                                pl.multiple_of(off + gi * cw, LANES)))

    def lora_cols(off):
        return pl.BlockSpec((1, L, LORA), lambda bi, gi, ci: (bi, ci, off // LORA))

    vec = pl.BlockSpec((1, cw), lambda bi, gi, ci: (0, gi))
    vec_lora = pl.BlockSpec((1, LORA), lambda bi, gi, ci: (0, 0))
    up = pl.BlockSpec((LORA, cw), lambda bi, gi, ci: (0, gi))
    kern = functools.partial(_rwkv_kernel, chunk=L, pairs=pairs)
    return pl.pallas_call(
        kern,
        grid=(b, RWKV_WIDTH // cw, s // L),
        in_specs=[cols(COL_R), cols(COL_K), cols(COL_V), lora_cols(COL_WLO), lora_cols(COL_ALO),
                  cols(COL_GA),
                  vec, vec, vec, vec_lora, vec_lora,
                  up, up, vec, vec, vec, vec, vec, vec, vec],
        out_specs=pl.BlockSpec((1, L, cw), lambda bi, gi, ci: (bi, ci, gi)),
        out_shape=jax.ShapeDtypeStruct((b, s, RWKV_WIDTH), BF16),
        scratch_shapes=[pltpu.VMEM((pairs, LANES, LANES), F32),
                        pltpu.VMEM((1, cw), F32), pltpu.VMEM((1, cw), F32),
                        pltpu.VMEM((1, cw), F32),
                        pltpu.VMEM((1, LORA), F32), pltpu.VMEM((1, LORA), F32)],
        compiler_params=_params(("arbitrary", "arbitrary", "arbitrary")),
        name="rwkv7_mix",
    )(p3, p3, p3, p3, p3, p3,
      row(mu[0:2048]), row(mu[2048:4096]), row(mu[4096:6144]), row(mu[6144:6272]),
      row(mu[6272:6400]),
      w_dec_up.astype(BF16), a_up.astype(BF16), row(w0), row(a0), row(k_k), row(k_a), row(r_k),
      row(gn_g), row(gn_b))


def _sb_kernel(q_ref, k_ref, v_ref, g_ref, o_ref, *, tq, td, tk):
    blk = SB_BLOCK
    s = q_ref.shape[1]
    ri = lax.broadcasted_iota(jnp.int32, (blk, blk), 0)
    ci = lax.broadcasted_iota(jnp.int32, (blk, blk), 1)
    mext = jnp.concatenate([jnp.where(ri > ci, 1.0, 0.0), jnp.ones((blk, blk), F32)],
                           axis=1).astype(BF16)
    causal = (lax.broadcasted_iota(jnp.int32, (td, td), 1)
              < lax.broadcasted_iota(jnp.int32, (td, td), 0))

    def tile(q, k0, width, carry, masked):
        kj = k_ref[0, k0:k0 + width, :]
        vj = v_ref[0, k0:k0 + width, :]
        z = _dot_nt(q, kj).astype(BF16)
        lq = jnp.minimum(z, 0.0) - jnp.log(1.0 + jnp.exp(-jnp.abs(z)))
        lk = lq - z
        if masked:
            lk = jnp.where(causal, lk, jnp.zeros_like(lk))
        later = []
        for i in reversed(range(width // blk)):
            sums = jnp.dot(lk[:, i * blk:(i + 1) * blk], mext, preferred_element_type=F32)
            later.append((sums[:, :blk] + carry).astype(BF16))
            carry = carry + sums[:, blk:]
        att = jnp.exp(lq + jnp.concatenate(later[::-1], axis=1))
        if masked:
            att = jnp.where(causal, att, jnp.zeros_like(att))
        return jnp.dot(att, vj, preferred_element_type=F32), carry

    for q0 in range(0, s, tq):
        accs, carries = [], []
        for rb in range(tq // td):
            q = q_ref[0, q0 + rb * td:q0 + (rb + 1) * td, :]
            acc = jnp.zeros((td, SB_HEAD), F32)
            carry = jnp.zeros((td, blk), F32)
            for kb in range(rb, -1, -1):
                part, carry = tile(q, q0 + kb * td, td, carry, kb == rb)
                acc = acc + part
            accs.append(acc)
            carries.append(carry)
        acc = jnp.concatenate(accs, axis=0)
        carry = jnp.concatenate(carries, axis=0)
        q = q_ref[0, q0:q0 + tq, :]
        for k0 in range(q0 - tk, -1, -tk):
            part, carry = tile(q, k0, tk, carry, False)
            acc = acc + part
        gate = _silu(g_ref[0, q0:q0 + tq, :].astype(F32))
        o_ref[0, q0:q0 + tq, :] = (acc * gate).astype(o_ref.dtype)


def _stick_breaking(p3):
    b, s, _ = p3.shape
    tq = min(SB_TQ, s)
    td = min(SB_TD, s)
    tk = min(SB_TK, s)
    assert tq % tk == 0 and tq % td == 0 and s % tq == 0

    def cols(off):
        return pl.BlockSpec((1, s, SB_HEAD), lambda bi, hi: (bi, 0, off // SB_HEAD + hi))

    kern = functools.partial(_sb_kernel, tq=tq, td=td, tk=tk)
    return pl.pallas_call(
        kern,
        grid=(b, SB_HEADS),
        in_specs=[cols(COL_Q), cols(COL_SK), cols(COL_SV), cols(COL_GB)],
        out_specs=pl.BlockSpec((1, s, SB_HEAD), lambda bi, hi: (bi, 0, hi)),
        out_shape=jax.ShapeDtypeStruct((b, s, SB_WIDTH), BF16),
        compiler_params=_params(("arbitrary", "arbitrary")),
        name="stick_breaking",
    )(p3, p3, p3, p3)


def kernel(x, norm_g, final_norm_g, e_w_in, e_shift_mu, e_w_decay_up, e_w0, e_a_up, e_a0,
           e_k_k, e_k_a, e_r_k, e_gn_g, e_gn_b, e_w_out, o_w_in, o_ln_g, o_ln_b, o_w_s,
           o_b_s, o_w_out):
    b, s, d = x.shape
    n = b * s
    x2 = x.reshape(n, d)
    assert e_w_in.shape[0] == 1 and o_w_in.shape[0] == 1

    w_in = e_w_in.reshape(d, -1).astype(BF16)
    h = _rmsnorm(x2, norm_g[0], BF16)
    pa, w_out_e, w_out_o = _matmul(h, w_in, 768, F32, _ep_plain, "even_in_proj_a", 0, EVEN_A_COLS,
                                   casts=(e_w_out.reshape(d, d), o_w_out.reshape(d, d)))
    q_tiles, q_scale = SB_WIDTH // 1024, float(SB_HEAD) ** -0.5
    pb, w_in_o = _matmul(h, w_in, 1024, BF16, lambda acc, j: acc * jnp.where(j < q_tiles, q_scale, 1.0),
                         "even_in_proj_b", EVEN_A_COLS, EVEN_B_COLS, casts=(o_w_in.reshape(d, -1),))
    ya = _rwkv(pa.reshape(b, s, EVEN_A_COLS), e_shift_mu[0], e_w_decay_up[0], e_w0[0], e_a_up[0],
               e_a0[0], e_k_k[0], e_k_a[0], e_r_k[0], e_gn_g[0], e_gn_b[0])
    yb = _stick_breaking(pb.reshape(b, s, EVEN_B_COLS))
    x2 = _matmul_res2(ya.reshape(n, RWKV_WIDTH), yb.reshape(n, SB_WIDTH), w_out_e, x2)

    h = _rmsnorm(x2, norm_g[1], BF16)
    uv = _matmul(h, w_in_o, 1024, BF16, _ep_gelu, "odd_in_proj_uv", 0, 2 * SGU_WIDTH)
    g = _matmul(h, w_in_o, 1024, BF16, _ep_silu, "odd_in_proj_g", 2 * SGU_WIDTH, SGU_WIDTH)
    out = _odd_tail(uv, g, o_ln_g[0], o_ln_b[0], o_w_s[0], o_b_s[0], w_out_o, x2, final_norm_g)
    return out.reshape(b, s, d)
```

```python
import functools
import math

import jax
import jax.numpy as jnp
from jax import lax
from jax.experimental import pallas as pl
from jax.experimental.pallas import tpu as pltpu

F32 = jnp.float32
BF16 = jnp.bfloat16

D_MODEL = 4096
RWKV_WIDTH = 2048
RWKV_HEAD = 64
LORA = 128
SB_WIDTH = 2048
SB_HEAD = 128
SB_HEADS = SB_WIDTH // SB_HEAD
SB_BLOCK = 128
SGU_WIDTH = 4096
SGU_CHUNK = 128
SGU_GROUPS = 16
SGU_GROUP_DIM = SGU_WIDTH // SGU_GROUPS

RMS_EPS = 1e-6
GN_EPS = 64e-5
LN_EPS = 1e-5
L2_EPS = 1e-12

LANES = 128
RWKV_CHUNK = 64
RWKV_PAIRS_PER_STEP = 16
SB_TQ = 512
SB_TD = 256
SB_TK = 512
CAST_ROWS = 32
OUT_TM = 512
OUT_TK = 512
OUT_TN = 1024
VMEM_LIMIT = 56 * 1024 * 1024

COL_R, COL_K, COL_V, COL_WLO, COL_ALO, COL_GA = 0, 2048, 4096, 6144, 6272, 6400
EVEN_A_COLS = 8448
COL_Q, COL_SK, COL_SV, COL_GB = 0, 2048, 4096, 6144
EVEN_B_COLS = 8192


def _params(sem):
    return pltpu.CompilerParams(dimension_semantics=sem, vmem_limit_bytes=VMEM_LIMIT)


def _dot(a, b):
    return jnp.dot(a.astype(BF16), b.astype(BF16), preferred_element_type=F32)


def _dot_nt(a, b):
    return lax.dot_general(a.astype(BF16), b.astype(BF16), (((1,), (1,)), ((), ())),
                           preferred_element_type=F32)


def _split(x):
    hi = x.astype(BF16)
    lo = (x - hi.astype(F32)).astype(BF16)
    return hi, lo


def _dot_exact_rhs(x, m):
    hi, lo = _split(x)
    return jnp.dot(jnp.concatenate([hi, lo], axis=1), jnp.concatenate([m, m], axis=0),
                   preferred_element_type=F32)


def _dot_exact_lhs(m, x):
    hi, lo = _split(x)
    return jnp.dot(jnp.concatenate([m, m], axis=1), jnp.concatenate([hi, lo], axis=0),
                   preferred_element_type=F32)


def _sigmoid(x):
    return 1.0 / (1.0 + jnp.exp(-x))


def _silu(x):
    return x * _sigmoid(x)


def _gelu(x):
    return 0.5 * x * (1.0 + lax.erf(x * (2.0 ** -0.5)))


def _rmsnorm_kernel(x_ref, g_ref, o_ref):
    x = x_ref[...]
    ms = jnp.mean(x * x, axis=-1, keepdims=True)
    o_ref[...] = (x * lax.rsqrt(ms + RMS_EPS) * g_ref[...]).astype(o_ref.dtype)


def _rmsnorm(x2, g, out_dtype):
    n, d = x2.shape
    tm = min(256, n)
    return pl.pallas_call(
        _rmsnorm_kernel,
        grid=(n // tm,),
        in_specs=[pl.BlockSpec((tm, d), lambda i: (i, 0)),
                  pl.BlockSpec((1, d), lambda i: (0, 0))],
        out_specs=pl.BlockSpec((tm, d), lambda i: (i, 0)),
        out_shape=jax.ShapeDtypeStruct((n, d), out_dtype),
        compiler_params=_params(("arbitrary",)),
        name="rmsnorm",
    )(x2, g.reshape(1, d))


def _ep_plain(acc, j):
    return acc


def _ep_gelu(acc, j):
    return _gelu(acc)


def _ep_silu(acc, j):
    return _silu(acc)


def _mm_kernel(*refs, epilogue, n_casts):
    a_ref, w_ref = refs[:2]
    srcs = refs[2:2 + n_casts]
    o_ref = refs[2 + n_casts]
    dsts = refs[3 + n_casts:]
    acc = jnp.dot(a_ref[...], w_ref[...], preferred_element_type=F32)
    o_ref[...] = epilogue(acc, pl.program_id(1)).astype(o_ref.dtype)
    for src, dst in zip(srcs, dsts):
        dst[...] = src[...].astype(dst.dtype)


def _matmul(a, w, tn, out_dtype, epilogue, name, col0=0, cols=None, casts=()):
    n, k = a.shape
    cols = w.shape[1] if cols is None else cols
    assert col0 % LANES == 0 and cols % tn == 0
    tm = min(1024, n)
    nj = cols // tn
    steps = (n // tm) * nj
    in_specs = [pl.BlockSpec((tm, k), lambda i, j: (i, 0)),
                pl.BlockSpec((pl.Element(k), pl.Element(tn)),
                             lambda i, j: (0, pl.multiple_of(col0 + j * tn, LANES)))]
    out_specs = [pl.BlockSpec((tm, tn), lambda i, j: (i, j))]
    out_shape = [jax.ShapeDtypeStruct((n, cols), out_dtype)]
    for src, c0, width in casts:
        rows = src.shape[0]
        nslab = 1 << (min(steps, rows // CAST_ROWS).bit_length() - 1)
        assert rows % nslab == 0 and c0 % LANES == 0
        srows = rows // nslab
        in_specs.append(pl.BlockSpec(
            (pl.Element(srows), pl.Element(width)),
            lambda i, j, nslab=nslab, srows=srows, c0=c0: (
                pl.multiple_of(jnp.minimum(i * nj + j, nslab - 1) * srows, srows), c0)))
        out_specs.append(pl.BlockSpec(
            (srows, width), lambda i, j, nslab=nslab: (jnp.minimum(i * nj + j, nslab - 1), 0)))
        out_shape.append(jax.ShapeDtypeStruct((rows, width), BF16))
    outs = pl.pallas_call(
        functools.partial(_mm_kernel, epilogue=epilogue, n_casts=len(casts)),
        grid=(n // tm, nj),
        in_specs=in_specs,
        out_specs=out_specs,
        out_shape=out_shape,
        compiler_params=_params(("arbitrary", "arbitrary")),
        name=name,
    )(a, w, *[src for src, _, _ in casts])
    return outs if casts else outs[0]


def _mm_res2_kernel(a_ref, b_ref, wa_ref, wb_ref, x_ref, o_ref):
    acc = jnp.dot(a_ref[...], wa_ref[...], preferred_element_type=F32)
    acc = acc + jnp.dot(b_ref[...], wb_ref[...], preferred_element_type=F32)
    o_ref[...] = x_ref[...] + acc


def _matmul_res2(a, b, w, x2):
    n, k = a.shape
    cols = w.shape[1]
    tm, tn = min(1024, n), 1024
    return pl.pallas_call(
        _mm_res2_kernel,
        grid=(n // tm, cols // tn),
        in_specs=[pl.BlockSpec((tm, k), lambda i, j: (i, 0)),
                  pl.BlockSpec((tm, k), lambda i, j: (i, 0)),
                  pl.BlockSpec((k, tn), lambda i, j: (0, j)),
                  pl.BlockSpec((k, tn), lambda i, j: (1, j)),
                  pl.BlockSpec((tm, tn), lambda i, j: (i, j))],
        out_specs=pl.BlockSpec((tm, tn), lambda i, j: (i, j)),
        out_shape=jax.ShapeDtypeStruct((n, cols), F32),
        compiler_params=_params(("arbitrary", "arbitrary")),
        name="even_out_proj",
    )(a, b, w, w, x2)


def _slabs(width):
    return [slice(c, c + OUT_TN) for c in range(0, width, OUT_TN)]


def _odd_tail_kernel(u_ref, v_ref, gate_ref, lng_ref, lnb_ref, ws_ref, bs_ref, w_ref, x_ref, fg_ref,
                     o_ref, mu_ref, rstd_ref, *, tk):
    k = pl.program_id(1)
    tm, d = o_ref.shape

    @pl.when(k == 0)
    def _():
        width = v_ref.shape[1]
        mu = sum(jnp.sum(v_ref[:, sl].astype(F32), axis=-1, keepdims=True)
                 for sl in _slabs(width)) * (1.0 / width)
        var = sum(jnp.sum(jnp.square(v_ref[:, sl].astype(F32) - mu), axis=-1, keepdims=True)
                  for sl in _slabs(width)) * (1.0 / width)
        mu_ref[...] = mu
        rstd_ref[...] = lax.rsqrt(var + LN_EPS)
        o_ref[...] = jnp.zeros_like(o_ref)

    vk = v_ref[:, pl.ds(pl.multiple_of(k * tk, tk), tk)].astype(F32)
    vn = ((vk - mu_ref[...]) * rstd_ref[...] * lng_ref[...] + lnb_ref[...]).astype(BF16)
    ri = lax.broadcasted_iota(jnp.int32, (SGU_CHUNK, SGU_CHUNK), 0)
    ci = lax.broadcasted_iota(jnp.int32, (SGU_CHUNK, SGU_CHUNK), 1)
    causal = ci <= ri
    cols = []
    for gg in range(tk // SGU_GROUP_DIM):
        grp = k * (tk // SGU_GROUP_DIM) + gg
        wmix = jnp.where(causal, ws_ref[grp], 0.0).astype(BF16)
        bias = bs_ref[grp]
        sl = slice(gg * SGU_GROUP_DIM, (gg + 1) * SGU_GROUP_DIM)
        rows = [jnp.dot(wmix, vn[c * SGU_CHUNK:(c + 1) * SGU_CHUNK, sl], preferred_element_type=F32) + bias
                for c in range(tm // SGU_CHUNK)]
        cols.append(jnp.concatenate(rows, axis=0))
    mixed = jnp.concatenate(cols, axis=1)
    y = (u_ref[...].astype(F32) * gate_ref[...].astype(F32) * mixed).astype(BF16)
    for sl in _slabs(d):
        o_ref[:, sl] += jnp.dot(y, w_ref[:, sl], preferred_element_type=F32)
    res_cols = pl.ds(pl.multiple_of(k * tk, tk), tk)
    o_ref[:, res_cols] += x_ref[...]

    @pl.when(k == pl.num_programs(1) - 1)
    def _():
        ss = sum(jnp.sum(jnp.square(o_ref[:, sl]), axis=-1, keepdims=True) for sl in _slabs(d))
        r = lax.rsqrt(ss * (1.0 / d) + RMS_EPS)
        for sl in _slabs(d):
            o_ref[:, sl] = o_ref[:, sl] * r * fg_ref[:, sl]


def _odd_tail(uv, gate, ln_g, ln_b, w_s, b_s, w, x2, fg):
    n, d = x2.shape
    tm, tk = min(OUT_TM, n), OUT_TK
    assert tm % SGU_CHUNK == 0 and tk % SGU_GROUP_DIM == 0 and d == SGU_WIDTH
    row = pl.BlockSpec((tm, d), lambda i, k: (i, 0))
    chunk = pl.BlockSpec((tm, tk), lambda i, k: (i, k))
    kvec = pl.BlockSpec((1, tk), lambda i, k: (0, k))
    return pl.pallas_call(
        functools.partial(_odd_tail_kernel, tk=tk),
        grid=(n // tm, SGU_WIDTH // tk),
        in_specs=[chunk,
                  pl.BlockSpec((tm, SGU_WIDTH), lambda i, k: (i, 1)),
                  chunk,
                  kvec, kvec,
                  pl.BlockSpec((SGU_GROUPS, SGU_CHUNK, SGU_CHUNK), lambda i, k: (0, 0, 0)),
                  pl.BlockSpec((SGU_GROUPS, SGU_CHUNK, 1), lambda i, k: (0, 0, 0)),
                  pl.BlockSpec((tk, d), lambda i, k: (k, 0)),
                  chunk,
                  pl.BlockSpec((1, d), lambda i, k: (0, 0))],
        out_specs=row,
        out_shape=jax.ShapeDtypeStruct((n, d), F32),
        scratch_shapes=[pltpu.VMEM((tm, 1), F32), pltpu.VMEM((tm, 1), F32)],
        compiler_params=_params(("arbitrary", "arbitrary")),
        name="odd_tail",
    )(uv, uv, gate, ln_g.reshape(1, -1), ln_b.reshape(1, -1), w_s, b_s[:, :, None], w, x2,
      fg.reshape(1, d))


def _bmm(a, b):
    return jnp.einsum("gij,gjk->gik", a.astype(BF16), b.astype(BF16), preferred_element_type=F32)


def _bmm_nt(a, b):
    return jnp.einsum("gik,gjk->gij", a.astype(BF16), b.astype(BF16), preferred_element_type=F32)


def _bmm_tn(a, b):
    return jnp.einsum("gki,gkj->gij", a.astype(BF16), b.astype(BF16), preferred_element_type=F32)


def _rwkv_kernel(pr_ref, pk_ref, pv_ref, pw_ref, pa_ref, g_ref,
                 mur_ref, muk_ref, muv_ref, muw_ref, mua_ref,
                 wup_ref, aup_ref, w0_ref, a0_ref, kk_ref, ka_ref, rk_ref,
                 gng_ref, gnb_ref,
                 o_ref,
                 s_ref, lr_ref, lk_ref, lv_ref, lw_ref, la_ref, *, chunk, pairs):
    L = chunk
    n = 2 * L
    c = pl.program_id(2)

    @pl.when(c == 0)
    def _():
        s_ref[...] = jnp.zeros_like(s_ref)
        lr_ref[...] = jnp.zeros_like(lr_ref)
        lk_ref[...] = jnp.zeros_like(lk_ref)
        lv_ref[...] = jnp.zeros_like(lv_ref)
        lw_ref[...] = jnp.zeros_like(lw_ref)
        la_ref[...] = jnp.zeros_like(la_ref)

    row = lax.broadcasted_iota(jnp.int32, (L, 1), 0)

    def shift_lerp(x_ref, last_ref, mu_ref):
        x = x_ref[0]
        prev = jnp.where(row == 0, last_ref[...], pltpu.roll(x, 1, 0))
        last_ref[...] = x[L - 1:L, :]
        return x + mu_ref[...] * (prev - x)

    ti = lax.broadcasted_iota(jnp.int32, (L, L), 0)
    tj = lax.broadcasted_iota(jnp.int32, (L, L), 1)
    tril_t = jnp.where(tj <= ti, 1.0, 0.0).astype(BF16)
    ei = lax.broadcasted_iota(jnp.int32, (LANES, LANES), 0)
    ej = lax.broadcasted_iota(jnp.int32, (LANES, LANES), 1)
    seg = jnp.where((ei // RWKV_HEAD) == (ej // RWKV_HEAD), 1.0, 0.0).astype(BF16)
    hmask = jnp.where((ei // L) == (ej // RWKV_HEAD), 1.0, 0.0).astype(BF16)
    strict = ej < ei
    incl = ej <= ei
    eye = jnp.where(ei == ej, 1.0, 0.0)

    def lane_tiles(x):
        return [x[:, i * LANES:(i + 1) * LANES] for i in range(pairs)]

    def head_sum(x):
        rows = jnp.concatenate(lane_tiles(x), axis=0)
        sums = _dot_exact_rhs(rows, seg)
        return jnp.concatenate([sums[i * L:(i + 1) * L] for i in range(pairs)], axis=1)

    def stack(x):
        xs = jnp.stack(lane_tiles(x.astype(BF16)), axis=0)
        return jnp.concatenate([xs, xs], axis=1) * hmask

    r = shift_lerp(pr_ref, lr_ref, mur_ref)
    k = shift_lerp(pk_ref, lk_ref, muk_ref)
    v = shift_lerp(pv_ref, lv_ref, muv_ref)
    wlo = shift_lerp(pw_ref, lw_ref, muw_ref)
    alo = shift_lerp(pa_ref, la_ref, mua_ref)

    wpre = w0_ref[...] + _dot(jnp.tanh(wlo), wup_ref[...])
    ld = -math.exp(-0.5) * _sigmoid(wpre)
    a = _sigmoid(a0_ref[...] + _dot(alo, aup_ref[...]))

    kk = k * kk_ref[...]
    kk = kk / jnp.maximum(jnp.sqrt(head_sum(kk * kk)), L2_EPS)
    k2 = k * (1.0 + (a - 1.0) * ka_ref[...])

    cum = _dot_exact_lhs(tril_t, ld)
    cum_last = cum[L - 1:L, :]
    w_incl = jnp.exp(cum)
    w_excl = jnp.exp(cum - ld)
    w_inv = jnp.exp(-cum)
    w_rest = jnp.exp(cum_last - cum)
    w_all = jnp.stack(lane_tiles(jnp.exp(cum_last)), axis=0)
    kb = kk * a

    a_s = stack(-kk * w_excl)
    r_s = stack(r * w_incl)
    b_s = stack(kb * w_inv)
    k_s = stack(k2 * w_inv)
    v_s = stack(v)
    bw_s = stack(kb * w_rest)
    kw_s = stack(k2 * w_rest)

    lhs = jnp.concatenate([a_s, r_s], axis=1)
    rhs = jnp.concatenate([b_s, k_s], axis=1)
    gram = _bmm_nt(lhs, rhs)
    a_ab = jnp.where(strict, gram[:, :n, :n], 0.0)
    a_ak = jnp.where(strict, gram[:, :n, n:], 0.0).astype(BF16)
    a_rb = jnp.where(incl, gram[:, n:, :n], 0.0).astype(BF16)
    a_rk = jnp.where(incl, gram[:, n:, n:], 0.0).astype(BF16)

    pw = a_ab.astype(BF16)
    inv = eye + a_ab
    for _ in range(L.bit_length() - 2):
        pw = _bmm(pw, pw).astype(BF16)
        inv = inv + _bmm(inv, pw)

    s_old = s_ref[...]
    t1 = _bmm_nt(lhs, s_old)
    u = _bmm(inv, t1[:, :n] + _bmm(a_ak, v_s))
    uv = jnp.concatenate([u.astype(BF16), v_s], axis=1)
    y_s = t1[:, n:] + _bmm(jnp.concatenate([a_rb, a_rk], axis=2), uv)
    s_ref[...] = s_old * w_all + _bmm_tn(uv, jnp.concatenate([bw_s, kw_s], axis=1))
    y3 = y_s[:, :L] + y_s[:, L:]
    y = jnp.concatenate([y3[i] for i in range(pairs)], axis=1)

    yc = y - head_sum(y) * (1.0 / RWKV_HEAD)
    var = head_sum(yc * yc) * (1.0 / RWKV_HEAD)
    yn = yc * lax.rsqrt(var + GN_EPS) * gng_ref[...] + gnb_ref[...]
    bonus = head_sum(r * k2 * rk_ref[...]) * v
    o_ref[0] = ((yn + bonus) * _silu(g_ref[0])).astype(o_ref.dtype)


def _rwkv(p3, mu, w_dec_up, w0, a_up, a0, k_k, k_a, r_k, gn_g, gn_b):
    b, s, _ = p3.shape
    L, pairs = RWKV_CHUNK, RWKV_PAIRS_PER_STEP
    assert 2 * L == LANES
    cw = pairs * LANES
    row = lambda t: t.reshape(1, -1)

    def cols(off):
        return pl.BlockSpec((pl.Element(1), pl.Element(L), pl.Element(cw)),
                            lambda bi, gi, ci: (bi, pl.multiple_of(ci * L, L),
                                                pl.multiple_of(off + gi * cw, LANES)))

    def lora_cols(off):
        return pl.BlockSpec((1, L, LORA), lambda bi, gi, ci: (bi, ci, off // LORA))

    vec = pl.BlockSpec((1, cw), lambda bi, gi, ci: (0, gi))
    vec_lora = pl.BlockSpec((1, LORA), lambda bi, gi, ci: (0, 0))
    up = pl.BlockSpec((LORA, cw), lambda bi, gi, ci: (0, gi))
    kern = functools.partial(_rwkv_kernel, chunk=L, pairs=pairs)
    return pl.pallas_call(
        kern,
        grid=(b, RWKV_WIDTH // cw, s // L),
        in_specs=[cols(COL_R), cols(COL_K), cols(COL_V), lora_cols(COL_WLO), lora_cols(COL_ALO),
                  cols(COL_GA),
                  vec, vec, vec, vec_lora, vec_lora,
                  up, up, vec, vec, vec, vec, vec, vec, vec],
        out_specs=pl.BlockSpec((1, L, cw), lambda bi, gi, ci: (bi, ci, gi)),
        out_shape=jax.ShapeDtypeStruct((b, s, RWKV_WIDTH), BF16),
        scratch_shapes=[pltpu.VMEM((pairs, LANES, LANES), F32),
                        pltpu.VMEM((1, cw), F32), pltpu.VMEM((1, cw), F32),
                        pltpu.VMEM((1, cw), F32),
                        pltpu.VMEM((1, LORA), F32), pltpu.VMEM((1, LORA), F32)],
        compiler_params=_params(("arbitrary", "arbitrary", "arbitrary")),
        name="rwkv7_mix",
    )(p3, p3, p3, p3, p3, p3,
      row(mu[0:2048]), row(mu[2048:4096]), row(mu[4096:6144]), row(mu[6144:6272]),
      row(mu[6272:6400]),
      w_dec_up.astype(BF16), a_up.astype(BF16), row(w0), row(a0), row(k_k), row(k_a), row(r_k),
      row(gn_g), row(gn_b))


def _sb_kernel(q_ref, k_ref, v_ref, g_ref, o_ref, *, tq, td, tk):
    blk = SB_BLOCK
    s = q_ref.shape[1]
    ri = lax.broadcasted_iota(jnp.int32, (blk, blk), 0)
    ci = lax.broadcasted_iota(jnp.int32, (blk, blk), 1)
    mext = jnp.concatenate([jnp.where(ri > ci, 1.0, 0.0), jnp.ones((blk, blk), F32)],
                           axis=1).astype(BF16)
    causal = (lax.broadcasted_iota(jnp.int32, (td, td), 1)
              < lax.broadcasted_iota(jnp.int32, (td, td), 0))

    def tile(q, k0, width, carry, masked):
        kj = k_ref[0, k0:k0 + width, :]
        vj = v_ref[0, k0:k0 + width, :]
        z = _dot_nt(q, kj).astype(BF16)
        lq = jnp.minimum(z, 0.0) - jnp.log(1.0 + jnp.exp(-jnp.abs(z)))
        lk = lq - z
        if masked:
            lk = jnp.where(causal, lk, jnp.zeros_like(lk))
        later = []
        for i in reversed(range(width // blk)):
            sums = jnp.dot(lk[:, i * blk:(i + 1) * blk], mext, preferred_element_type=F32)
            later.append((sums[:, :blk] + carry).astype(BF16))
            carry = carry + sums[:, blk:]
        att = jnp.exp(lq + jnp.concatenate(later[::-1], axis=1))
        if masked:
            att = jnp.where(causal, att, jnp.zeros_like(att))
        return jnp.dot(att, vj, preferred_element_type=F32), carry

    for q0 in range(0, s, tq):
        accs, carries = [], []
        for rb in range(tq // td):
            q = q_ref[0, q0 + rb * td:q0 + (rb + 1) * td, :]
            acc = jnp.zeros((td, SB_HEAD), F32)
            carry = jnp.zeros((td, blk), F32)
            for kb in range(rb, -1, -1):
                part, carry = tile(q, q0 + kb * td, td, carry, kb == rb)
                acc = acc + part
            accs.append(acc)
            carries.append(carry)
        acc = jnp.concatenate(accs, axis=0)
        carry = jnp.concatenate(carries, axis=0)
        q = q_ref[0, q0:q0 + tq, :]
        for k0 in range(q0 - tk, -1, -tk):
            part, carry = tile(q, k0, tk, carry, False)
            acc = acc + part
        gate = _silu(g_ref[0, q0:q0 + tq, :].astype(F32))
        o_ref[0, q0:q0 + tq, :] = (acc * gate).astype(o_ref.dtype)


def _stick_breaking(p3):
    b, s, _ = p3.shape
    tq = min(SB_TQ, s)
    td = min(SB_TD, s)
    tk = min(SB_TK, s)
    assert tq % tk == 0 and tq % td == 0 and s % tq == 0

    def cols(off):
        return pl.BlockSpec((1, s, SB_HEAD), lambda bi, hi: (bi, 0, off // SB_HEAD + hi))

    kern = functools.partial(_sb_kernel, tq=tq, td=td, tk=tk)
    return pl.pallas_call(
        kern,
        grid=(b, SB_HEADS),
        in_specs=[cols(COL_Q), cols(COL_SK), cols(COL_SV), cols(COL_GB)],
        out_specs=pl.BlockSpec((1, s, SB_HEAD), lambda bi, hi: (bi, 0, hi)),
        out_shape=jax.ShapeDtypeStruct((b, s, SB_WIDTH), BF16),
        compiler_params=_params(("arbitrary", "arbitrary")),
        name="stick_breaking",
    )(p3, p3, p3, p3)


def kernel(x, norm_g, final_norm_g, e_w_in, e_shift_mu, e_w_decay_up, e_w0, e_a_up, e_a0,
           e_k_k, e_k_a, e_r_k, e_gn_g, e_gn_b, e_w_out, o_w_in, o_ln_g, o_ln_b, o_w_s,
           o_b_s, o_w_out):
    b, s, d = x.shape
    n = b * s
    x2 = x.reshape(n, d)
    assert e_w_in.shape[0] == 1 and o_w_in.shape[0] == 1

    w_in = e_w_in.reshape(d, -1)
    h = _rmsnorm(x2, norm_g[0], BF16)
    pa, w_in_b, w_out_e, w_out_o = _matmul(
        h, w_in[:, :EVEN_A_COLS].astype(BF16), 768, F32, _ep_plain, "even_in_proj_a",
        casts=((w_in, EVEN_A_COLS, EVEN_B_COLS), (e_w_out.reshape(d, d), 0, d), (o_w_out.reshape(d, d), 0, d)))
    q_tiles, q_scale = SB_WIDTH // 1024, float(SB_HEAD) ** -0.5
    pb, w_in_o = _matmul(h, w_in_b, 1024, BF16, lambda acc, j: acc * jnp.where(j < q_tiles, q_scale, 1.0),
                         "even_in_proj_b", casts=((o_w_in.reshape(d, -1), 0, o_w_in.shape[-1]),))
    ya = _rwkv(pa.reshape(b, s, EVEN_A_COLS), e_shift_mu[0], e_w_decay_up[0], e_w0[0], e_a_up[0],
               e_a0[0], e_k_k[0], e_k_a[0], e_r_k[0], e_gn_g[0], e_gn_b[0])
    yb = _stick_breaking(pb.reshape(b, s, EVEN_B_COLS))
    x2 = _matmul_res2(ya.reshape(n, RWKV_WIDTH), yb.reshape(n, SB_WIDTH), w_out_e, x2)

    h = _rmsnorm(x2, norm_g[1], BF16)
    uv = _matmul(h, w_in_o, 1024, BF16, _ep_gelu, "odd_in_proj_uv", 0, 2 * SGU_WIDTH)
    g = _matmul(h, w_in_o, 1024, BF16, _ep_silu, "odd_in_proj_g", 2 * SGU_WIDTH, SGU_WIDTH)
    out = _odd_tail(uv, g, o_ln_g[0], o_ln_b[0], o_w_s[0], o_b_s[0], w_out_o, x2, final_norm_g)
    return out.reshape(b, s, d)
```

```python
import functools
import math

import jax
import jax.numpy as jnp
from jax import lax
from jax.experimental import pallas as pl
from jax.experimental.pallas import tpu as pltpu

F32 = jnp.float32
BF16 = jnp.bfloat16

D_MODEL = 4096
RWKV_WIDTH = 2048
RWKV_HEAD = 64
LORA = 128
SB_WIDTH = 2048
SB_HEAD = 128
SB_HEADS = SB_WIDTH // SB_HEAD
SB_BLOCK = 128
SGU_WIDTH = 4096
SGU_CHUNK = 128
SGU_GROUPS = 16
SGU_GROUP_DIM = SGU_WIDTH // SGU_GROUPS

RMS_EPS = 1e-6
GN_EPS = 64e-5
LN_EPS = 1e-5
L2_EPS = 1e-12

LANES = 128
RWKV_CHUNK = 64
RWKV_PAIRS_PER_STEP = 16
SB_TQ = 512
SB_TD = 256
SB_TK = 512
CAST_ROWS = 32
OUT_TM = 512
OUT_TK = 512
OUT_TN = 512
VMEM_LIMIT = 56 * 1024 * 1024

COL_R, COL_K, COL_V, COL_WLO, COL_ALO, COL_GA = 0, 2048, 4096, 6144, 6272, 6400
EVEN_A_COLS = 8448
COL_Q, COL_SK, COL_SV, COL_GB = 0, 2048, 4096, 6144
EVEN_B_COLS = 8192


def _params(sem):
    return pltpu.CompilerParams(dimension_semantics=sem, vmem_limit_bytes=VMEM_LIMIT)


def _dot(a, b):
    return jnp.dot(a.astype(BF16), b.astype(BF16), preferred_element_type=F32)


def _dot_nt(a, b):
    return lax.dot_general(a.astype(BF16), b.astype(BF16), (((1,), (1,)), ((), ())),
                           preferred_element_type=F32)


def _split(x):
    hi = x.astype(BF16)
    lo = (x - hi.astype(F32)).astype(BF16)
    return hi, lo


def _dot_exact_rhs(x, m):
    hi, lo = _split(x)
    return jnp.dot(jnp.concatenate([hi, lo], axis=1), jnp.concatenate([m, m], axis=0),
                   preferred_element_type=F32)


def _dot_exact_lhs(m, x):
    hi, lo = _split(x)
    return jnp.dot(jnp.concatenate([m, m], axis=1), jnp.concatenate([hi, lo], axis=0),
                   preferred_element_type=F32)


def _sigmoid(x):
    return 1.0 / (1.0 + jnp.exp(-x))


def _silu(x):
    return x * _sigmoid(x)


def _gelu(x):
    return 0.5 * x * (1.0 + lax.erf(x * (2.0 ** -0.5)))


def _rmsnorm_kernel(x_ref, g_ref, o_ref):
    x = x_ref[...]
    ms = jnp.mean(x * x, axis=-1, keepdims=True)
    o_ref[...] = (x * lax.rsqrt(ms + RMS_EPS) * g_ref[...]).astype(o_ref.dtype)


def _rmsnorm(x2, g, out_dtype):
    n, d = x2.shape
    tm = min(256, n)
    return pl.pallas_call(
        _rmsnorm_kernel,
        grid=(n // tm,),
        in_specs=[pl.BlockSpec((tm, d), lambda i: (i, 0)),
                  pl.BlockSpec((1, d), lambda i: (0, 0))],
        out_specs=pl.BlockSpec((tm, d), lambda i: (i, 0)),
        out_shape=jax.ShapeDtypeStruct((n, d), out_dtype),
        compiler_params=_params(("arbitrary",)),
        name="rmsnorm",
    )(x2, g.reshape(1, d))


def _ep_plain(acc, j):
    return acc


def _ep_gelu(acc, j):
    return _gelu(acc)


def _ep_silu(acc, j):
    return _silu(acc)


def _mm_kernel(*refs, epilogue, n_casts):
    a_ref, w_ref = refs[:2]
    srcs = refs[2:2 + n_casts]
    o_ref = refs[2 + n_casts]
    dsts = refs[3 + n_casts:]
    acc = jnp.dot(a_ref[...], w_ref[...], preferred_element_type=F32)
    o_ref[...] = epilogue(acc, pl.program_id(1)).astype(o_ref.dtype)
    for src, dst in zip(srcs, dsts):
        dst[...] = src[...].astype(dst.dtype)


def _matmul(a, w, tn, out_dtype, epilogue, name, col0=0, cols=None, casts=()):
    n, k = a.shape
    cols = w.shape[1] if cols is None else cols
    assert col0 % LANES == 0 and cols % tn == 0
    tm = min(1024, n)
    nj = cols // tn
    steps = (n // tm) * nj
    in_specs = [pl.BlockSpec((tm, k), lambda i, j: (i, 0)),
                pl.BlockSpec((pl.Element(k), pl.Element(tn)),
                             lambda i, j: (0, pl.multiple_of(col0 + j * tn, LANES)))]
    out_specs = [pl.BlockSpec((tm, tn), lambda i, j: (i, j))]
    out_shape = [jax.ShapeDtypeStruct((n, cols), out_dtype)]
    for src, c0, width in casts:
        rows = src.shape[0]
        nslab = 1 << (min(steps, rows // CAST_ROWS).bit_length() - 1)
        assert rows % nslab == 0 and c0 % LANES == 0
        srows = rows // nslab
        in_specs.append(pl.BlockSpec(
            (pl.Element(srows), pl.Element(width)),
            lambda i, j, nslab=nslab, srows=srows, c0=c0: (
                pl.multiple_of(jnp.minimum(i * nj + j, nslab - 1) * srows, srows), c0)))
        out_specs.append(pl.BlockSpec(
            (srows, width), lambda i, j, nslab=nslab: (jnp.minimum(i * nj + j, nslab - 1), 0)))
        out_shape.append(jax.ShapeDtypeStruct((rows, width), BF16))
    outs = pl.pallas_call(
        functools.partial(_mm_kernel, epilogue=epilogue, n_casts=len(casts)),
        grid=(n // tm, nj),
        in_specs=in_specs,
        out_specs=out_specs,
        out_shape=out_shape,
        compiler_params=_params(("arbitrary", "arbitrary")),
        name=name,
    )(a, w, *[src for src, _, _ in casts])
    return outs if casts else outs[0]


def _mm_res2_kernel(a_ref, b_ref, wa_ref, wb_ref, x_ref, g_ref, o_ref, h_ref, row_ref):
    j = pl.program_id(1)
    tn = o_ref.shape[1]
    acc = jnp.dot(a_ref[...], wa_ref[...], preferred_element_type=F32)
    acc = acc + jnp.dot(b_ref[...], wb_ref[...], preferred_element_type=F32)
    o = x_ref[...] + acc
    o_ref[...] = o
    row_ref[:, pl.ds(pl.multiple_of(j * tn, tn), tn)] = o

    @pl.when(j == pl.num_programs(1) - 1)
    def _():
        d = row_ref.shape[1]
        ss = sum(jnp.sum(jnp.square(row_ref[:, sl]), axis=-1, keepdims=True) for sl in _slabs(d))
        r = lax.rsqrt(ss * (1.0 / d) + RMS_EPS)
        for sl in _slabs(d):
            h_ref[:, sl] = (row_ref[:, sl] * r * g_ref[:, sl]).astype(h_ref.dtype)


def _matmul_res2(a, b, w, x2, g):
    n, k = a.shape
    cols = w.shape[1]
    tm, tn = min(512, n), 1024
    tile = pl.BlockSpec((tm, tn), lambda i, j: (i, j))
    return pl.pallas_call(
        _mm_res2_kernel,
        grid=(n // tm, cols // tn),
        in_specs=[pl.BlockSpec((tm, k), lambda i, j: (i, 0)),
                  pl.BlockSpec((tm, k), lambda i, j: (i, 0)),
                  pl.BlockSpec((k, tn), lambda i, j: (0, j)),
                  pl.BlockSpec((k, tn), lambda i, j: (1, j)),
                  tile,
                  pl.BlockSpec((1, cols), lambda i, j: (0, 0))],
        out_specs=[tile, pl.BlockSpec((tm, cols), lambda i, j: (i, 0))],
        out_shape=[jax.ShapeDtypeStruct((n, cols), F32), jax.ShapeDtypeStruct((n, cols), BF16)],
        scratch_shapes=[pltpu.VMEM((tm, cols), F32)],
        compiler_params=_params(("arbitrary", "arbitrary")),
        name="even_out_proj",
    )(a, b, w, w, x2, g.reshape(1, cols))


def _slabs(width):
    return [slice(c, c + OUT_TN) for c in range(0, width, OUT_TN)]


def _odd_tail_kernel(u_ref, v_ref, gate_ref, lng_ref, lnb_ref, ws_ref, bs_ref, w_ref, x_ref, fg_ref,
                     o_ref, mu_ref, rstd_ref, *, tk):
    k = pl.program_id(1)
    tm, d = o_ref.shape

    @pl.when(k == 0)
    def _():
        width = v_ref.shape[1]
        mu = sum(jnp.sum(v_ref[:, sl].astype(F32), axis=-1, keepdims=True)
                 for sl in _slabs(width)) * (1.0 / width)
        var = sum(jnp.sum(jnp.square(v_ref[:, sl].astype(F32) - mu), axis=-1, keepdims=True)
                  for sl in _slabs(width)) * (1.0 / width)
        mu_ref[...] = mu
        rstd_ref[...] = lax.rsqrt(var + LN_EPS)
        o_ref[...] = jnp.zeros_like(o_ref)

    vk = v_ref[:, pl.ds(pl.multiple_of(k * tk, tk), tk)].astype(F32)
    vn = ((vk - mu_ref[...]) * rstd_ref[...] * lng_ref[...] + lnb_ref[...]).astype(BF16)
    ri = lax.broadcasted_iota(jnp.int32, (SGU_CHUNK, SGU_CHUNK), 0)
    ci = lax.broadcasted_iota(jnp.int32, (SGU_CHUNK, SGU_CHUNK), 1)
    causal = ci <= ri
    cols = []
    for gg in range(tk // SGU_GROUP_DIM):
        grp = k * (tk // SGU_GROUP_DIM) + gg
        wmix = jnp.where(causal, ws_ref[grp], 0.0).astype(BF16)
        bias = bs_ref[grp]
        sl = slice(gg * SGU_GROUP_DIM, (gg + 1) * SGU_GROUP_DIM)
        rows = [jnp.dot(wmix, vn[c * SGU_CHUNK:(c + 1) * SGU_CHUNK, sl], preferred_element_type=F32) + bias
                for c in range(tm // SGU_CHUNK)]
        cols.append(jnp.concatenate(rows, axis=0))
    mixed = jnp.concatenate(cols, axis=1)
    y = (u_ref[...].astype(F32) * gate_ref[...].astype(F32) * mixed).astype(BF16)
    for sl in _slabs(d):
        o_ref[:, sl] += jnp.dot(y, w_ref[:, sl], preferred_element_type=F32)
    res_cols = pl.ds(pl.multiple_of(k * tk, tk), tk)
    o_ref[:, res_cols] += x_ref[...]

    @pl.when(k == pl.num_programs(1) - 1)
    def _():
        ss = sum(jnp.sum(jnp.square(o_ref[:, sl]), axis=-1, keepdims=True) for sl in _slabs(d))
        r = lax.rsqrt(ss * (1.0 / d) + RMS_EPS)
        for sl in _slabs(d):
            o_ref[:, sl] = o_ref[:, sl] * r * fg_ref[:, sl]


def _odd_tail(uv, gate, ln_g, ln_b, w_s, b_s, w, x2, fg):
    n, d = x2.shape
    tm, tk = min(OUT_TM, n), OUT_TK
    assert tm % SGU_CHUNK == 0 and tk % SGU_GROUP_DIM == 0 and d == SGU_WIDTH
    row = pl.BlockSpec((tm, d), lambda i, k: (i, 0))
    chunk = pl.BlockSpec((tm, tk), lambda i, k: (i, k))
    kvec = pl.BlockSpec((1, tk), lambda i, k: (0, k))
    return pl.pallas_call(
        functools.partial(_odd_tail_kernel, tk=tk),
        grid=(n // tm, SGU_WIDTH // tk),
        in_specs=[chunk,
                  pl.BlockSpec((tm, SGU_WIDTH), lambda i, k: (i, 1)),
                  chunk,
                  kvec, kvec,
                  pl.BlockSpec((SGU_GROUPS, SGU_CHUNK, SGU_CHUNK), lambda i, k: (0, 0, 0)),
                  pl.BlockSpec((SGU_GROUPS, SGU_CHUNK, 1), lambda i, k: (0, 0, 0)),
                  pl.BlockSpec((tk, d), lambda i, k: (k, 0)),
                  chunk,
                  pl.BlockSpec((1, d), lambda i, k: (0, 0))],
        out_specs=row,
        out_shape=jax.ShapeDtypeStruct((n, d), F32),
        scratch_shapes=[pltpu.VMEM((tm, 1), F32), pltpu.VMEM((tm, 1), F32)],
        compiler_params=_params(("arbitrary", "arbitrary")),
        name="odd_tail",
    )(uv, uv, gate, ln_g.reshape(1, -1), ln_b.reshape(1, -1), w_s, b_s[:, :, None], w, x2,
      fg.reshape(1, d))


def _bmm(a, b):
    return jnp.einsum("gij,gjk->gik", a.astype(BF16), b.astype(BF16), preferred_element_type=F32)


def _bmm_nt(a, b):
    return jnp.einsum("gik,gjk->gij", a.astype(BF16), b.astype(BF16), preferred_element_type=F32)


def _bmm_tn(a, b):
    return jnp.einsum("gki,gkj->gij", a.astype(BF16), b.astype(BF16), preferred_element_type=F32)


def _rwkv_kernel(pr_ref, pk_ref, pv_ref, pw_ref, pa_ref, g_ref,
                 mur_ref, muk_ref, muv_ref, muw_ref, mua_ref,
                 wup_ref, aup_ref, w0_ref, a0_ref, kk_ref, ka_ref, rk_ref,
                 gng_ref, gnb_ref,
                 o_ref,
                 s_ref, lr_ref, lk_ref, lv_ref, lw_ref, la_ref, *, chunk, pairs):
    L = chunk
    n = 2 * L
    c = pl.program_id(2)

    @pl.when(c == 0)
    def _():
        s_ref[...] = jnp.zeros_like(s_ref)
        lr_ref[...] = jnp.zeros_like(lr_ref)
        lk_ref[...] = jnp.zeros_like(lk_ref)
        lv_ref[...] = jnp.zeros_like(lv_ref)
        lw_ref[...] = jnp.zeros_like(lw_ref)
        la_ref[...] = jnp.zeros_like(la_ref)

    row = lax.broadcasted_iota(jnp.int32, (L, 1), 0)

    def shift_lerp(x_ref, last_ref, mu_ref):
        x = x_ref[0]
        prev = jnp.where(row == 0, last_ref[...], pltpu.roll(x, 1, 0))
        last_ref[...] = x[L - 1:L, :]
        return x + mu_ref[...] * (prev - x)

    ti = lax.broadcasted_iota(jnp.int32, (L, L), 0)
    tj = lax.broadcasted_iota(jnp.int32, (L, L), 1)
    tril_t = jnp.where(tj <= ti, 1.0, 0.0).astype(BF16)
    ei = lax.broadcasted_iota(jnp.int32, (LANES, LANES), 0)
    ej = lax.broadcasted_iota(jnp.int32, (LANES, LANES), 1)
    seg = jnp.where((ei // RWKV_HEAD) == (ej // RWKV_HEAD), 1.0, 0.0).astype(BF16)
    hmask = jnp.where((ei // L) == (ej // RWKV_HEAD), 1.0, 0.0).astype(BF16)
    strict = ej < ei
    incl = ej <= ei
    eye = jnp.where(ei == ej, 1.0, 0.0)

    def lane_tiles(x):
        return [x[:, i * LANES:(i + 1) * LANES] for i in range(pairs)]

    def head_sum(x):
        rows = jnp.concatenate(lane_tiles(x), axis=0)
        sums = _dot_exact_rhs(rows, seg)
        return jnp.concatenate([sums[i * L:(i + 1) * L] for i in range(pairs)], axis=1)

    def stack(x):
        xs = jnp.stack(lane_tiles(x.astype(BF16)), axis=0)
        return jnp.concatenate([xs, xs], axis=1) * hmask

    r = shift_lerp(pr_ref, lr_ref, mur_ref)
    k = shift_lerp(pk_ref, lk_ref, muk_ref)
    v = shift_lerp(pv_ref, lv_ref, muv_ref)
    wlo = shift_lerp(pw_ref, lw_ref, muw_ref)
    alo = shift_lerp(pa_ref, la_ref, mua_ref)

    wpre = w0_ref[...] + _dot(jnp.tanh(wlo), wup_ref[...])
    ld = -math.exp(-0.5) * _sigmoid(wpre)
    a = _sigmoid(a0_ref[...] + _dot(alo, aup_ref[...]))

    kk = k * kk_ref[...]
    kk = kk / jnp.maximum(jnp.sqrt(head_sum(kk * kk)), L2_EPS)
    k2 = k * (1.0 + (a - 1.0) * ka_ref[...])

    cum = _dot_exact_lhs(tril_t, ld)
    cum_last = cum[L - 1:L, :]
    w_incl = jnp.exp(cum)
    w_excl = jnp.exp(cum - ld)
    w_inv = jnp.exp(-cum)
    w_rest = jnp.exp(cum_last - cum)
    w_all = jnp.stack(lane_tiles(jnp.exp(cum_last)), axis=0)
    kb = kk * a

    a_s = stack(-kk * w_excl)
    r_s = stack(r * w_incl)
    b_s = stack(kb * w_inv)
    k_s = stack(k2 * w_inv)
    v_s = stack(v)
    bw_s = stack(kb * w_rest)
    kw_s = stack(k2 * w_rest)

    lhs = jnp.concatenate([a_s, r_s], axis=1)
    rhs = jnp.concatenate([b_s, k_s], axis=1)
    gram = _bmm_nt(lhs, rhs)
    a_ab = jnp.where(strict, gram[:, :n, :n], 0.0)
    a_ak = jnp.where(strict, gram[:, :n, n:], 0.0).astype(BF16)
    a_rb = jnp.where(incl, gram[:, n:, :n], 0.0).astype(BF16)
    a_rk = jnp.where(incl, gram[:, n:, n:], 0.0).astype(BF16)

    pw = a_ab.astype(BF16)
    inv = eye + a_ab
    for _ in range(L.bit_length() - 2):
        pw = _bmm(pw, pw).astype(BF16)
        inv = inv + _bmm(inv, pw)

    s_old = s_ref[...]
    t1 = _bmm_nt(lhs, s_old)
    u = _bmm(inv, t1[:, :n] + _bmm(a_ak, v_s))
    uv = jnp.concatenate([u.astype(BF16), v_s], axis=1)
    y_s = t1[:, n:] + _bmm(jnp.concatenate([a_rb, a_rk], axis=2), uv)
    s_ref[...] = s_old * w_all + _bmm_tn(uv, jnp.concatenate([bw_s, kw_s], axis=1))
    y3 = y_s[:, :L] + y_s[:, L:]
    y = jnp.concatenate([y3[i] for i in range(pairs)], axis=1)

    yc = y - head_sum(y) * (1.0 / RWKV_HEAD)
    var = head_sum(yc * yc) * (1.0 / RWKV_HEAD)
    yn = yc * lax.rsqrt(var + GN_EPS) * gng_ref[...] + gnb_ref[...]
    bonus = head_sum(r * k2 * rk_ref[...]) * v
    o_ref[0] = ((yn + bonus) * _silu(g_ref[0])).astype(o_ref.dtype)


def _rwkv(p3, mu, w_dec_up, w0, a_up, a0, k_k, k_a, r_k, gn_g, gn_b):
    b, s, _ = p3.shape
    L, pairs = RWKV_CHUNK, RWKV_PAIRS_PER_STEP
    assert 2 * L == LANES
    cw = pairs * LANES
    row = lambda t: t.reshape(1, -1)

    def cols(off):
        return pl.BlockSpec((pl.Element(1), pl.Element(L), pl.Element(cw)),
                            lambda bi, gi, ci: (bi, pl.multiple_of(ci * L, L),
                                                pl.multiple_of(off + gi * cw, LANES)))

    def lora_cols(off):
        return pl.BlockSpec((1, L, LORA), lambda bi, gi, ci: (bi, ci, off // LORA))

    vec = pl.BlockSpec((1, cw), lambda bi, gi, ci: (0, gi))
    vec_lora = pl.BlockSpec((1, LORA), lambda bi, gi, ci: (0, 0))
    up = pl.BlockSpec((LORA, cw), lambda bi, gi, ci: (0, gi))
    kern = functools.partial(_rwkv_kernel, chunk=L, pairs=pairs)
    return pl.pallas_call(
        kern,
        grid=(b, RWKV_WIDTH // cw, s // L),
        in_specs=[cols(COL_R), cols(COL_K), cols(COL_V), lora_cols(COL_WLO), lora_cols(COL_ALO),
                  cols(COL_GA),
                  vec, vec, vec, vec_lora, vec_lora,
                  up, up, vec, vec, vec, vec, vec, vec, vec],
        out_specs=pl.BlockSpec((1, L, cw), lambda bi, gi, ci: (bi, ci, gi)),
        out_shape=jax.ShapeDtypeStruct((b, s, RWKV_WIDTH), BF16),
        scratch_shapes=[pltpu.VMEM((pairs, LANES, LANES), F32),
                        pltpu.VMEM((1, cw), F32), pltpu.VMEM((1, cw), F32),
                        pltpu.VMEM((1, cw), F32),
                        pltpu.VMEM((1, LORA), F32), pltpu.VMEM((1, LORA), F32)],
        compiler_params=_params(("arbitrary", "arbitrary", "arbitrary")),
        name="rwkv7_mix",
    )(p3, p3, p3, p3, p3, p3,
      row(mu[0:2048]), row(mu[2048:4096]), row(mu[4096:6144]), row(mu[6144:6272]),
      row(mu[6272:6400]),
      w_dec_up.astype(BF16), a_up.astype(BF16), row(w0), row(a0), row(k_k), row(k_a), row(r_k),
      row(gn_g), row(gn_b))


def _sb_kernel(q_ref, k_ref, v_ref, g_ref, o_ref, *, tq, td, tk):
    blk = SB_BLOCK
    s = q_ref.shape[1]
    ri = lax.broadcasted_iota(jnp.int32, (blk, blk), 0)
    ci = lax.broadcasted_iota(jnp.int32, (blk, blk), 1)
    mext = jnp.concatenate([jnp.where(ri > ci, 1.0, 0.0), jnp.ones((blk, blk), F32)],
                           axis=1).astype(BF16)
    causal = (lax.broadcasted_iota(jnp.int32, (td, td), 1)
              < lax.broadcasted_iota(jnp.int32, (td, td), 0))

    def tile(q, k0, width, carry, masked):
        kj = k_ref[0, k0:k0 + width, :]
        vj = v_ref[0, k0:k0 + width, :]
        z = _dot_nt(q, kj).astype(BF16)
        lq = jnp.minimum(z, 0.0) - jnp.log(1.0 + jnp.exp(-jnp.abs(z)))
        lk = lq - z
        if masked:
            lk = jnp.where(causal, lk, jnp.zeros_like(lk))
        later = []
        for i in reversed(range(width // blk)):
            sums = jnp.dot(lk[:, i * blk:(i + 1) * blk], mext, preferred_element_type=F32)
            later.append((sums[:, :blk] + carry).astype(BF16))
            carry = carry + sums[:, blk:]
        att = jnp.exp(lq + jnp.concatenate(later[::-1], axis=1))
        if masked:
            att = jnp.where(causal, att, jnp.zeros_like(att))
        return jnp.dot(att, vj, preferred_element_type=F32), carry

    for q0 in range(0, s, tq):
        accs, carries = [], []
        for rb in range(tq // td):
            q = q_ref[0, q0 + rb * td:q0 + (rb + 1) * td, :]
            acc = jnp.zeros((td, SB_HEAD), F32)
            carry = jnp.zeros((td, blk), F32)
            for kb in range(rb, -1, -1):
                part, carry = tile(q, q0 + kb * td, td, carry, kb == rb)
                acc = acc + part
            accs.append(acc)
            carries.append(carry)
        acc = jnp.concatenate(accs, axis=0)
        carry = jnp.concatenate(carries, axis=0)
        q = q_ref[0, q0:q0 + tq, :]
        for k0 in range(q0 - tk, -1, -tk):
            part, carry = tile(q, k0, tk, carry, False)
            acc = acc + part
        gate = _silu(g_ref[0, q0:q0 + tq, :].astype(F32))
        o_ref[0, q0:q0 + tq, :] = (acc * gate).astype(o_ref.dtype)


def _stick_breaking(p3):
    b, s, _ = p3.shape
    tq = min(SB_TQ, s)
    td = min(SB_TD, s)
    tk = min(SB_TK, s)
    assert tq % tk == 0 and tq % td == 0 and s % tq == 0

    def cols(off):
        return pl.BlockSpec((1, s, SB_HEAD), lambda bi, hi: (bi, 0, off // SB_HEAD + hi))

    kern = functools.partial(_sb_kernel, tq=tq, td=td, tk=tk)
    return pl.pallas_call(
        kern,
        grid=(b, SB_HEADS),
        in_specs=[cols(COL_Q), cols(COL_SK), cols(COL_SV), cols(COL_GB)],
        out_specs=pl.BlockSpec((1, s, SB_HEAD), lambda bi, hi: (bi, 0, hi)),
        out_shape=jax.ShapeDtypeStruct((b, s, SB_WIDTH), BF16),
        compiler_params=_params(("arbitrary", "arbitrary")),
        name="stick_breaking",
    )(p3, p3, p3, p3)


def kernel(x, norm_g, final_norm_g, e_w_in, e_shift_mu, e_w_decay_up, e_w0, e_a_up, e_a0,
           e_k_k, e_k_a, e_r_k, e_gn_g, e_gn_b, e_w_out, o_w_in, o_ln_g, o_ln_b, o_w_s,
           o_b_s, o_w_out):
    b, s, d = x.shape
    n = b * s
    x2 = x.reshape(n, d)
    assert e_w_in.shape[0] == 1 and o_w_in.shape[0] == 1

    w_in = e_w_in.reshape(d, -1)
    h = _rmsnorm(x2, norm_g[0], BF16)
    pa, w_in_b, w_out_e, w_out_o = _matmul(
        h, w_in[:, :EVEN_A_COLS].astype(BF16), 768, F32, _ep_plain, "even_in_proj_a",
        casts=((w_in, EVEN_A_COLS, EVEN_B_COLS), (e_w_out.reshape(d, d), 0, d), (o_w_out.reshape(d, d), 0, d)))
    q_tiles, q_scale = SB_WIDTH // 1024, float(SB_HEAD) ** -0.5
    pb, w_in_o = _matmul(h, w_in_b, 1024, BF16, lambda acc, j: acc * jnp.where(j < q_tiles, q_scale, 1.0),
                         "even_in_proj_b", casts=((o_w_in.reshape(d, -1), 0, o_w_in.shape[-1]),))
    ya = _rwkv(pa.reshape(b, s, EVEN_A_COLS), e_shift_mu[0], e_w_decay_up[0], e_w0[0], e_a_up[0],
               e_a0[0], e_k_k[0], e_k_a[0], e_r_k[0], e_gn_g[0], e_gn_b[0])
    yb = _stick_breaking(pb.reshape(b, s, EVEN_B_COLS))
    x2, h = _matmul_res2(ya.reshape(n, RWKV_WIDTH), yb.reshape(n, SB_WIDTH), w_out_e, x2, norm_g[1])

    uv = _matmul(h, w_in_o, 1024, BF16, _ep_gelu, "odd_in_proj_uv", 0, 2 * SGU_WIDTH)
    g = _matmul(h, w_in_o, 1024, BF16, _ep_silu, "odd_in_proj_g", 2 * SGU_WIDTH, SGU_WIDTH)
    out = _odd_tail(uv, g, o_ln_g[0], o_ln_b[0], o_w_s[0], o_b_s[0], w_out_o, x2, final_norm_g)
    return out.reshape(b, s, d)
```

```python
import functools
import math

import jax
import jax.numpy as jnp
from jax import lax
from jax.experimental import pallas as pl
from jax.experimental.pallas import tpu as pltpu

F32 = jnp.float32
BF16 = jnp.bfloat16

D_MODEL = 4096
RWKV_WIDTH = 2048
RWKV_HEAD = 64
LORA = 128
SB_WIDTH = 2048
SB_HEAD = 128
SB_HEADS = SB_WIDTH // SB_HEAD
SB_BLOCK = 128
SGU_WIDTH = 4096
SGU_CHUNK = 128
SGU_GROUPS = 16
SGU_GROUP_DIM = SGU_WIDTH // SGU_GROUPS

RMS_EPS = 1e-6
GN_EPS = 64e-5
LN_EPS = 1e-5
L2_EPS = 1e-12

LANES = 128
RWKV_CHUNK = 64
RWKV_PAIRS_PER_STEP = 16
SB_TQ = 512
SB_TD = 256
SB_TK = 512
CAST_ROWS = 32
OUT_TM = 512
OUT_TK = 512
OUT_TN = 1024
VMEM_LIMIT = 56 * 1024 * 1024

COL_R, COL_K, COL_V, COL_WLO, COL_ALO, COL_GA = 0, 2048, 4096, 6144, 6272, 6400
EVEN_A_COLS = 8448
COL_Q, COL_SK, COL_SV, COL_GB = 0, 2048, 4096, 6144
EVEN_B_COLS = 8192


def _params(sem):
    return pltpu.CompilerParams(dimension_semantics=sem, vmem_limit_bytes=VMEM_LIMIT)


def _dot(a, b):
    return jnp.dot(a.astype(BF16), b.astype(BF16), preferred_element_type=F32)


def _dot_nt(a, b):
    return lax.dot_general(a.astype(BF16), b.astype(BF16), (((1,), (1,)), ((), ())),
                           preferred_element_type=F32)


def _split(x):
    hi = x.astype(BF16)
    lo = (x - hi.astype(F32)).astype(BF16)
    return hi, lo


def _dot_exact_rhs(x, m):
    hi, lo = _split(x)
    return jnp.dot(jnp.concatenate([hi, lo], axis=1), jnp.concatenate([m, m], axis=0),
                   preferred_element_type=F32)


def _dot_exact_lhs(m, x):
    hi, lo = _split(x)
    return jnp.dot(jnp.concatenate([m, m], axis=1), jnp.concatenate([hi, lo], axis=0),
                   preferred_element_type=F32)


def _sigmoid(x):
    return 1.0 / (1.0 + jnp.exp(-x))


def _silu(x):
    return x * _sigmoid(x)


def _gelu(x):
    return 0.5 * x * (1.0 + lax.erf(x * (2.0 ** -0.5)))


def _rmsnorm_kernel(x_ref, g_ref, o_ref):
    x = x_ref[...]
    ms = jnp.mean(x * x, axis=-1, keepdims=True)
    o_ref[...] = (x * lax.rsqrt(ms + RMS_EPS) * g_ref[...]).astype(o_ref.dtype)


def _rmsnorm(x2, g, out_dtype):
    n, d = x2.shape
    tm = min(256, n)
    return pl.pallas_call(
        _rmsnorm_kernel,
        grid=(n // tm,),
        in_specs=[pl.BlockSpec((tm, d), lambda i: (i, 0)),
                  pl.BlockSpec((1, d), lambda i: (0, 0))],
        out_specs=pl.BlockSpec((tm, d), lambda i: (i, 0)),
        out_shape=jax.ShapeDtypeStruct((n, d), out_dtype),
        compiler_params=_params(("arbitrary",)),
        name="rmsnorm",
    )(x2, g.reshape(1, d))


def _ep_plain(acc, j):
    return acc


def _ep_gelu(acc, j):
    return _gelu(acc)


def _ep_silu(acc, j):
    return _silu(acc)


def _mm_kernel(*refs, epilogue, n_casts):
    a_ref, w_ref = refs[:2]
    srcs = refs[2:2 + n_casts]
    o_ref = refs[2 + n_casts]
    dsts = refs[3 + n_casts:]
    acc = jnp.dot(a_ref[...], w_ref[...], preferred_element_type=F32)
    o_ref[...] = epilogue(acc, pl.program_id(1)).astype(o_ref.dtype)
    for src, dst in zip(srcs, dsts):
        dst[...] = src[...].astype(dst.dtype)


def _matmul(a, w, tn, out_dtype, epilogue, name, col0=0, cols=None, casts=()):
    n, k = a.shape
    cols = w.shape[1] if cols is None else cols
    assert col0 % LANES == 0 and cols % tn == 0
    tm = min(1024, n)
    nj = cols // tn
    steps = (n // tm) * nj
    in_specs = [pl.BlockSpec((tm, k), lambda i, j: (i, 0)),
                pl.BlockSpec((pl.Element(k), pl.Element(tn)),
                             lambda i, j: (0, pl.multiple_of(col0 + j * tn, LANES)))]
    out_specs = [pl.BlockSpec((tm, tn), lambda i, j: (i, j))]
    out_shape = [jax.ShapeDtypeStruct((n, cols), out_dtype)]
    for src, c0, width in casts:
        rows = src.shape[0]
        nslab = 1 << (min(steps, rows // CAST_ROWS).bit_length() - 1)
        assert rows % nslab == 0 and c0 % LANES == 0
        srows = rows // nslab
        in_specs.append(pl.BlockSpec(
            (pl.Element(srows), pl.Element(width)),
            lambda i, j, nslab=nslab, srows=srows, c0=c0: (
                pl.multiple_of(jnp.minimum(i * nj + j, nslab - 1) * srows, srows), c0)))
        out_specs.append(pl.BlockSpec(
            (srows, width), lambda i, j, nslab=nslab: (jnp.minimum(i * nj + j, nslab - 1), 0)))
        out_shape.append(jax.ShapeDtypeStruct((rows, width), BF16))
    outs = pl.pallas_call(
        functools.partial(_mm_kernel, epilogue=epilogue, n_casts=len(casts)),
        grid=(n // tm, nj),
        in_specs=in_specs,
        out_specs=out_specs,
        out_shape=out_shape,
        compiler_params=_params(("arbitrary", "arbitrary")),
        name=name,
    )(a, w, *[src for src, _, _ in casts])
    return outs if casts else outs[0]


def _mm_res2_kernel(a_ref, b_ref, wa_ref, wb_ref, x_ref, o_ref):
    acc = jnp.dot(a_ref[...], wa_ref[...], preferred_element_type=F32)
    acc = acc + jnp.dot(b_ref[...], wb_ref[...], preferred_element_type=F32)
    o_ref[...] = x_ref[...] + acc


def _matmul_res2(a, b, w, x2):
    n, k = a.shape
    cols = w.shape[1]
    tm, tn = min(1024, n), 1024
    return pl.pallas_call(
        _mm_res2_kernel,
        grid=(n // tm, cols // tn),
        in_specs=[pl.BlockSpec((tm, k), lambda i, j: (i, 0)),
                  pl.BlockSpec((tm, k), lambda i, j: (i, 0)),
                  pl.BlockSpec((k, tn), lambda i, j: (0, j)),
                  pl.BlockSpec((k, tn), lambda i, j: (1, j)),
                  pl.BlockSpec((tm, tn), lambda i, j: (i, j))],
        out_specs=pl.BlockSpec((tm, tn), lambda i, j: (i, j)),
        out_shape=jax.ShapeDtypeStruct((n, cols), F32),
        compiler_params=_params(("arbitrary", "arbitrary")),
        name="even_out_proj",
    )(a, b, w, w, x2)


def _slabs(width):
    return [slice(c, c + OUT_TN) for c in range(0, width, OUT_TN)]


def _odd_tail_kernel(u_ref, v_ref, gate_ref, lng_ref, lnb_ref, ws_ref, bs_ref, w_ref, x_ref, fg_ref,
                     o_ref, mu_ref, rstd_ref, *, tk):
    k = pl.program_id(1)
    tm, d = o_ref.shape

    @pl.when(k == 0)
    def _():
        width = v_ref.shape[1]
        mu = sum(jnp.sum(v_ref[:, sl].astype(F32), axis=-1, keepdims=True)
                 for sl in _slabs(width)) * (1.0 / width)
        var = sum(jnp.sum(jnp.square(v_ref[:, sl].astype(F32) - mu), axis=-1, keepdims=True)
                  for sl in _slabs(width)) * (1.0 / width)
        mu_ref[...] = mu
        rstd_ref[...] = lax.rsqrt(var + LN_EPS)
        o_ref[...] = jnp.zeros_like(o_ref)

    vk = v_ref[:, pl.ds(pl.multiple_of(k * tk, tk), tk)].astype(F32)
    vn = ((vk - mu_ref[...]) * rstd_ref[...] * lng_ref[...] + lnb_ref[...]).astype(BF16)
    ri = lax.broadcasted_iota(jnp.int32, (SGU_CHUNK, SGU_CHUNK), 0)
    ci = lax.broadcasted_iota(jnp.int32, (SGU_CHUNK, SGU_CHUNK), 1)
    causal = ci <= ri
    cols = []
    for gg in range(tk // SGU_GROUP_DIM):
        grp = k * (tk // SGU_GROUP_DIM) + gg
        wmix = jnp.where(causal, ws_ref[grp], 0.0).astype(BF16)
        bias = bs_ref[grp]
        sl = slice(gg * SGU_GROUP_DIM, (gg + 1) * SGU_GROUP_DIM)
        rows = [jnp.dot(wmix, vn[c * SGU_CHUNK:(c + 1) * SGU_CHUNK, sl], preferred_element_type=F32) + bias
                for c in range(tm // SGU_CHUNK)]
        cols.append(jnp.concatenate(rows, axis=0))
    mixed = jnp.concatenate(cols, axis=1)
    y = (u_ref[...].astype(F32) * gate_ref[...].astype(F32) * mixed).astype(BF16)
    for sl in _slabs(d):
        o_ref[:, sl] += jnp.dot(y, w_ref[:, sl], preferred_element_type=F32)
    res_cols = pl.ds(pl.multiple_of(k * tk, tk), tk)
    o_ref[:, res_cols] += x_ref[...]

    @pl.when(k == pl.num_programs(1) - 1)
    def _():
        ss = sum(jnp.sum(jnp.square(o_ref[:, sl]), axis=-1, keepdims=True) for sl in _slabs(d))
        r = lax.rsqrt(ss * (1.0 / d) + RMS_EPS)
        for sl in _slabs(d):
            o_ref[:, sl] = o_ref[:, sl] * r * fg_ref[:, sl]


def _odd_tail(uv, gate, ln_g, ln_b, w_s, b_s, w, x2, fg):
    n, d = x2.shape
    tm, tk = min(OUT_TM, n), OUT_TK
    assert tm % SGU_CHUNK == 0 and tk % SGU_GROUP_DIM == 0 and d == SGU_WIDTH
    row = pl.BlockSpec((tm, d), lambda i, k: (i, 0))
    chunk = pl.BlockSpec((tm, tk), lambda i, k: (i, k))
    kvec = pl.BlockSpec((1, tk), lambda i, k: (0, k))
    return pl.pallas_call(
        functools.partial(_odd_tail_kernel, tk=tk),
        grid=(n // tm, SGU_WIDTH // tk),
        in_specs=[chunk,
                  pl.BlockSpec((tm, SGU_WIDTH), lambda i, k: (i, 1)),
                  chunk,
                  kvec, kvec,
                  pl.BlockSpec((SGU_GROUPS, SGU_CHUNK, SGU_CHUNK), lambda i, k: (0, 0, 0)),
                  pl.BlockSpec((SGU_GROUPS, SGU_CHUNK, 1), lambda i, k: (0, 0, 0)),
                  pl.BlockSpec((tk, d), lambda i, k: (k, 0)),
                  chunk,
                  pl.BlockSpec((1, d), lambda i, k: (0, 0))],
        out_specs=row,
        out_shape=jax.ShapeDtypeStruct((n, d), F32),
        scratch_shapes=[pltpu.VMEM((tm, 1), F32), pltpu.VMEM((tm, 1), F32)],
        compiler_params=_params(("arbitrary", "arbitrary")),
        name="odd_tail",
    )(uv, uv, gate, ln_g.reshape(1, -1), ln_b.reshape(1, -1), w_s, b_s[:, :, None], w, x2,
      fg.reshape(1, d))


def _bmm(a, b):
    return jnp.einsum("gij,gjk->gik", a.astype(BF16), b.astype(BF16), preferred_element_type=F32)


def _bmm_nt(a, b):
    return jnp.einsum("gik,gjk->gij", a.astype(BF16), b.astype(BF16), preferred_element_type=F32)


def _bmm_tn(a, b):
    return jnp.einsum("gki,gkj->gij", a.astype(BF16), b.astype(BF16), preferred_element_type=F32)


def _rwkv_kernel(pr_ref, pk_ref, pv_ref, pw_ref, pa_ref, g_ref,
                 mur_ref, muk_ref, muv_ref, muw_ref, mua_ref,
                 wup_ref, aup_ref, w0_ref, a0_ref, kk_ref, ka_ref, rk_ref,
                 gng_ref, gnb_ref,
                 o_ref,
                 s_ref, lr_ref, lk_ref, lv_ref, lw_ref, la_ref, *, chunk, pairs, nchunks, first):
    @pl.when(first)
    def _():
        s_ref[...] = jnp.zeros_like(s_ref)
        lr_ref[...] = jnp.zeros_like(lr_ref)
        lk_ref[...] = jnp.zeros_like(lk_ref)
        lv_ref[...] = jnp.zeros_like(lv_ref)
        lw_ref[...] = jnp.zeros_like(lw_ref)
        la_ref[...] = jnp.zeros_like(la_ref)

    for ci in range(nchunks):
        _rwkv_chunk(pr_ref, pk_ref, pv_ref, pw_ref, pa_ref, g_ref,
                    mur_ref, muk_ref, muv_ref, muw_ref, mua_ref,
                    wup_ref, aup_ref, w0_ref, a0_ref, kk_ref, ka_ref, rk_ref, gng_ref, gnb_ref,
                    o_ref, s_ref, lr_ref, lk_ref, lv_ref, lw_ref, la_ref,
                    chunk=chunk, pairs=pairs, rows=slice(ci * chunk, (ci + 1) * chunk))


def _rwkv_chunk(pr_ref, pk_ref, pv_ref, pw_ref, pa_ref, g_ref,
                mur_ref, muk_ref, muv_ref, muw_ref, mua_ref,
                wup_ref, aup_ref, w0_ref, a0_ref, kk_ref, ka_ref, rk_ref, gng_ref, gnb_ref,
                o_ref, s_ref, lr_ref, lk_ref, lv_ref, lw_ref, la_ref, *, chunk, pairs, rows):
    L = chunk
    n = 2 * L
    row = lax.broadcasted_iota(jnp.int32, (L, 1), 0)

    def shift_lerp(x_ref, last_ref, mu_ref):
        x = x_ref[0, rows, :]
        prev = jnp.where(row == 0, last_ref[...], pltpu.roll(x, 1, 0))
        last_ref[...] = x[L - 1:L, :]
        return x + mu_ref[...] * (prev - x)

    ti = lax.broadcasted_iota(jnp.int32, (L, L), 0)
    tj = lax.broadcasted_iota(jnp.int32, (L, L), 1)
    tril_t = jnp.where(tj <= ti, 1.0, 0.0).astype(BF16)
    ei = lax.broadcasted_iota(jnp.int32, (LANES, LANES), 0)
    ej = lax.broadcasted_iota(jnp.int32, (LANES, LANES), 1)
    seg = jnp.where((ei // RWKV_HEAD) == (ej // RWKV_HEAD), 1.0, 0.0).astype(BF16)
    hmask = jnp.where((ei // L) == (ej // RWKV_HEAD), 1.0, 0.0).astype(BF16)
    strict = ej < ei
    incl = ej <= ei
    eye = jnp.where(ei == ej, 1.0, 0.0)

    def lane_tiles(x):
        return [x[:, i * LANES:(i + 1) * LANES] for i in range(pairs)]

    def head_sum(x):
        rows = jnp.concatenate(lane_tiles(x), axis=0)
        sums = _dot_exact_rhs(rows, seg)
        return jnp.concatenate([sums[i * L:(i + 1) * L] for i in range(pairs)], axis=1)

    def stack(x):
        xs = jnp.stack(lane_tiles(x.astype(BF16)), axis=0)
        return jnp.concatenate([xs, xs], axis=1) * hmask

    r = shift_lerp(pr_ref, lr_ref, mur_ref)
    k = shift_lerp(pk_ref, lk_ref, muk_ref)
    v = shift_lerp(pv_ref, lv_ref, muv_ref)
    wlo = shift_lerp(pw_ref, lw_ref, muw_ref)
    alo = shift_lerp(pa_ref, la_ref, mua_ref)

    wpre = w0_ref[...] + _dot(jnp.tanh(wlo), wup_ref[...])
    ld = -math.exp(-0.5) * _sigmoid(wpre)
    a = _sigmoid(a0_ref[...] + _dot(alo, aup_ref[...]))

    kk = k * kk_ref[...]
    kk = kk / jnp.maximum(jnp.sqrt(head_sum(kk * kk)), L2_EPS)
    k2 = k * (1.0 + (a - 1.0) * ka_ref[...])

    cum = _dot_exact_lhs(tril_t, ld)
    cum_last = cum[L - 1:L, :]
    w_incl = jnp.exp(cum)
    w_excl = jnp.exp(cum - ld)
    w_inv = jnp.exp(-cum)
    w_rest = jnp.exp(cum_last - cum)
    w_all = jnp.stack(lane_tiles(jnp.exp(cum_last)), axis=0)
    kb = kk * a

    a_s = stack(-kk * w_excl)
    r_s = stack(r * w_incl)
    b_s = stack(kb * w_inv)
    k_s = stack(k2 * w_inv)
    v_s = stack(v)
    bw_s = stack(kb * w_rest)
    kw_s = stack(k2 * w_rest)

    lhs = jnp.concatenate([a_s, r_s], axis=1)
    rhs = jnp.concatenate([b_s, k_s], axis=1)
    gram = _bmm_nt(lhs, rhs)
    a_ab = jnp.where(strict, gram[:, :n, :n], 0.0)
    a_ak = jnp.where(strict, gram[:, :n, n:], 0.0).astype(BF16)
    a_rb = jnp.where(incl, gram[:, n:, :n], 0.0).astype(BF16)
    a_rk = jnp.where(incl, gram[:, n:, n:], 0.0).astype(BF16)

    pw = a_ab.astype(BF16)
    inv = eye + a_ab
    for _ in range(L.bit_length() - 2):
        pw = _bmm(pw, pw).astype(BF16)
        inv = inv + _bmm(inv, pw)

    s_old = s_ref[...]
    t1 = _bmm_nt(lhs, s_old)
    u = _bmm(inv, t1[:, :n] + _bmm(a_ak, v_s))
    uv = jnp.concatenate([u.astype(BF16), v_s], axis=1)
    y_s = t1[:, n:] + _bmm(jnp.concatenate([a_rb, a_rk], axis=2), uv)
    s_ref[...] = s_old * w_all + _bmm_tn(uv, jnp.concatenate([bw_s, kw_s], axis=1))
    y3 = y_s[:, :L] + y_s[:, L:]
    y = jnp.concatenate([y3[i] for i in range(pairs)], axis=1)

    yc = y - head_sum(y) * (1.0 / RWKV_HEAD)
    var = head_sum(yc * yc) * (1.0 / RWKV_HEAD)
    yn = yc * lax.rsqrt(var + GN_EPS) * gng_ref[...] + gnb_ref[...]
    bonus = head_sum(r * k2 * rk_ref[...]) * v
    o_ref[0, rows, :] = ((yn + bonus) * _silu(g_ref[0, rows, :])).astype(o_ref.dtype)


def _sb_kernel(q_ref, k_ref, v_ref, g_ref, o_ref, *, tq, td, tk):
    blk = SB_BLOCK
    s = q_ref.shape[1]
    ri = lax.broadcasted_iota(jnp.int32, (blk, blk), 0)
    ci = lax.broadcasted_iota(jnp.int32, (blk, blk), 1)
    mext = jnp.concatenate([jnp.where(ri > ci, 1.0, 0.0), jnp.ones((blk, blk), F32)],
                           axis=1).astype(BF16)
    causal = (lax.broadcasted_iota(jnp.int32, (td, td), 1)
              < lax.broadcasted_iota(jnp.int32, (td, td), 0))

    def tile(q, k0, width, carry, masked):
        kj = k_ref[0, k0:k0 + width, :]
        vj = v_ref[0, k0:k0 + width, :]
        z = _dot_nt(q, kj).astype(BF16)
        lq = jnp.minimum(z, 0.0) - jnp.log(1.0 + jnp.exp(-jnp.abs(z)))
        lk = lq - z
        if masked:
            lk = jnp.where(causal, lk, jnp.zeros_like(lk))
        later = []
        for i in reversed(range(width // blk)):
            sums = jnp.dot(lk[:, i * blk:(i + 1) * blk], mext, preferred_element_type=F32)
            later.append((sums[:, :blk] + carry).astype(BF16))
            carry = carry + sums[:, blk:]
        att = jnp.exp(lq + jnp.concatenate(later[::-1], axis=1))
        if masked:
            att = jnp.where(causal, att, jnp.zeros_like(att))
        return jnp.dot(att, vj, preferred_element_type=F32), carry

    for q0 in range(0, s, tq):
        accs, carries = [], []
        for rb in range(tq // td):
            q = q_ref[0, q0 + rb * td:q0 + (rb + 1) * td, :]
            acc = jnp.zeros((td, SB_HEAD), F32)
            carry = jnp.zeros((td, blk), F32)
            for kb in range(rb, -1, -1):
                part, carry = tile(q, q0 + kb * td, td, carry, kb == rb)
                acc = acc + part
            accs.append(acc)
            carries.append(carry)
        acc = jnp.concatenate(accs, axis=0)
        carry = jnp.concatenate(carries, axis=0)
        q = q_ref[0, q0:q0 + tq, :]
        for k0 in range(q0 - tk, -1, -tk):
            part, carry = tile(q, k0, tk, carry, False)
            acc = acc + part
        gate = _silu(g_ref[0, q0:q0 + tq, :].astype(F32))
        o_ref[0, q0:q0 + tq, :] = (acc * gate).astype(o_ref.dtype)


def _mixers_kernel(*refs, chunk, pairs, nchunks, tq, td, tk):
    rw_in, sb_in = refs[:20], refs[20:24]
    ya_ref, yb_ref = refs[24:26]
    scratch = refs[26:]
    _rwkv_kernel(*rw_in, ya_ref, *scratch, chunk=chunk, pairs=pairs, nchunks=nchunks,
                 first=pl.program_id(1) == 0)
    _sb_kernel(*sb_in, yb_ref, tq=tq, td=td, tk=tk)


def _mixers(pa3, pb3, mu, w_dec_up, w0, a_up, a0, k_k, k_a, r_k, gn_g, gn_b):
    b, s, _ = pa3.shape
    L, pairs = RWKV_CHUNK, RWKV_PAIRS_PER_STEP
    cw = pairs * LANES
    assert 2 * L == LANES and cw == RWKV_WIDTH and s % (SB_HEADS * L) == 0
    nchunks = s // (SB_HEADS * L)
    rl = nchunks * L
    tq, td, tk = min(SB_TQ, s), min(SB_TD, s), min(SB_TK, s)
    assert tq % tk == 0 and tq % td == 0 and s % tq == 0
    row = lambda t: t.reshape(1, -1)

    def cols(off):
        return pl.BlockSpec((pl.Element(1), pl.Element(rl), pl.Element(cw)),
                            lambda bi, hi: (bi, pl.multiple_of(hi * rl, rl), off))

    def lora_cols(off):
        return pl.BlockSpec((1, rl, LORA), lambda bi, hi: (bi, hi, off // LORA))

    def sb_cols(off):
        return pl.BlockSpec((1, s, SB_HEAD), lambda bi, hi: (bi, 0, off // SB_HEAD + hi))

    vec = pl.BlockSpec((1, cw), lambda bi, hi: (0, 0))
    vec_lora = pl.BlockSpec((1, LORA), lambda bi, hi: (0, 0))
    up = pl.BlockSpec((LORA, cw), lambda bi, hi: (0, 0))
    kern = functools.partial(_mixers_kernel, chunk=L, pairs=pairs, nchunks=nchunks, tq=tq, td=td, tk=tk)
    return pl.pallas_call(
        kern,
        grid=(b, SB_HEADS),
        in_specs=[cols(COL_R), cols(COL_K), cols(COL_V), lora_cols(COL_WLO), lora_cols(COL_ALO),
                  cols(COL_GA),
                  vec, vec, vec, vec_lora, vec_lora,
                  up, up, vec, vec, vec, vec, vec, vec, vec,
                  sb_cols(COL_Q), sb_cols(COL_SK), sb_cols(COL_SV), sb_cols(COL_GB)],
        out_specs=[pl.BlockSpec((1, rl, cw), lambda bi, hi: (bi, hi, 0)),
                   pl.BlockSpec((1, s, SB_HEAD), lambda bi, hi: (bi, 0, hi))],
        out_shape=[jax.ShapeDtypeStruct((b, s, RWKV_WIDTH), BF16),
                   jax.ShapeDtypeStruct((b, s, SB_WIDTH), BF16)],
        scratch_shapes=[pltpu.VMEM((pairs, LANES, LANES), F32),
                        pltpu.VMEM((1, cw), F32), pltpu.VMEM((1, cw), F32),
                        pltpu.VMEM((1, cw), F32),
                        pltpu.VMEM((1, LORA), F32), pltpu.VMEM((1, LORA), F32)],
        compiler_params=_params(("arbitrary", "arbitrary")),
        name="mixers",
    )(pa3, pa3, pa3, pa3, pa3, pa3,
      row(mu[0:2048]), row(mu[2048:4096]), row(mu[4096:6144]), row(mu[6144:6272]),
      row(mu[6272:6400]),
      w_dec_up.astype(BF16), a_up.astype(BF16), row(w0), row(a0), row(k_k), row(k_a), row(r_k),
      row(gn_g), row(gn_b), pb3, pb3, pb3, pb3)


def kernel(x, norm_g, final_norm_g, e_w_in, e_shift_mu, e_w_decay_up, e_w0, e_a_up, e_a0,
           e_k_k, e_k_a, e_r_k, e_gn_g, e_gn_b, e_w_out, o_w_in, o_ln_g, o_ln_b, o_w_s,
           o_b_s, o_w_out):
    b, s, d = x.shape
    n = b * s
    x2 = x.reshape(n, d)
    assert e_w_in.shape[0] == 1 and o_w_in.shape[0] == 1

    w_in = e_w_in.reshape(d, -1)
    h = _rmsnorm(x2, norm_g[0], BF16)
    pa, w_in_b, w_out_e, w_out_o = _matmul(
        h, w_in[:, :EVEN_A_COLS].astype(BF16), 768, F32, _ep_plain, "even_in_proj_a",
        casts=((w_in, EVEN_A_COLS, EVEN_B_COLS), (e_w_out.reshape(d, d), 0, d), (o_w_out.reshape(d, d), 0, d)))
    q_tiles, q_scale = SB_WIDTH // 1024, float(SB_HEAD) ** -0.5
    pb, w_in_o = _matmul(h, w_in_b, 1024, BF16, lambda acc, j: acc * jnp.where(j < q_tiles, q_scale, 1.0),
                         "even_in_proj_b", casts=((o_w_in.reshape(d, -1), 0, o_w_in.shape[-1]),))
    ya, yb = _mixers(pa.reshape(b, s, EVEN_A_COLS), pb.reshape(b, s, EVEN_B_COLS), e_shift_mu[0],
                     e_w_decay_up[0], e_w0[0], e_a_up[0], e_a0[0], e_k_k[0], e_k_a[0], e_r_k[0],
                     e_gn_g[0], e_gn_b[0])
    x2 = _matmul_res2(ya.reshape(n, RWKV_WIDTH), yb.reshape(n, SB_WIDTH), w_out_e, x2)

    h = _rmsnorm(x2, norm_g[1], BF16)
    uv = _matmul(h, w_in_o, 1024, BF16, _ep_gelu, "odd_in_proj_uv", 0, 2 * SGU_WIDTH)
    g = _matmul(h, w_in_o, 1024, BF16, _ep_silu, "odd_in_proj_g", 2 * SGU_WIDTH, SGU_WIDTH)
    out = _odd_tail(uv, g, o_ln_g[0], o_ln_b[0], o_w_s[0], o_b_s[0], w_out_o, x2, final_norm_g)
    return out.reshape(b, s, d)
```
